```python
import jax
import jax.numpy as jnp
from jax import lax
import numpy as np

D_MODEL = 1024
BATCH = 4
SEQ = 8192
DEPTH = 2

GRID_W = 64
CTX_LEN = 256
N_MIXERS = 2
N_ATT = (DEPTH + 1) // 2
N_RWKV = DEPTH // 2
HEAD_DIM = 64
ATT_Q_HEADS = D_MODEL // HEAD_DIM
ATT_KV_HEADS = ATT_Q_HEADS // 4
ATT_GROUP = ATT_Q_HEADS // ATT_KV_HEADS
WINDOW = 128
BLOCK = 128
ROPE_BASE = 10000.0
RWKV_HEADS = D_MODEL // HEAD_DIM
DECAY_LORA = 64
ICLR_LORA = 64
GATE_LORA = 128
D_FF = 4 * D_MODEL
NORM_EPS = 1e-6
GN_EPS = 64e-5
NEG_INF = -1e30
F32 = jnp.float32

kernel_name = 'hybrid_swa_rwkv7_dit_layers'


def rmsnorm(x, g):
    x32 = x.astype(F32)
    y = x32 * lax.rsqrt(jnp.mean(x32 * x32, axis=-1, keepdims=True) + NORM_EPS)
    return (y * g.astype(F32)).astype(x.dtype)


def rope_1d(t, pos):
    half = t.shape[-1] // 2
    freqs = ROPE_BASE ** (-jnp.arange(half, dtype=F32) / half)
    ang = pos.astype(F32)[:, None, None] * freqs
    cos, sin = jnp.cos(ang), jnp.sin(ang)
    t1, t2 = t[..., :half].astype(F32), t[..., half:].astype(F32)
    return jnp.concatenate([t1 * cos - t2 * sin, t1 * sin + t2 * cos], axis=-1).astype(t.dtype)


def rope_2d(t, rows, cols):
    r = HEAD_DIM // 2
    return jnp.concatenate([rope_1d(t[..., :r], rows), rope_1d(t[..., r:], cols)], axis=-1)


def sq_relu_mlp(h, w1, w2):
    return jnp.square(jax.nn.relu(h @ w1)) @ w2


def attention_mixer(h, hc, w_qkv, q_gain, k_gain, sink, w_o, rows, cols, ctx_out):
    B, S, _ = h.shape
    L = hc.shape[1]
    nq, nk = ATT_Q_HEADS * HEAD_DIM, ATT_KV_HEADS * HEAD_DIM

    def project(t):
        T = t.shape[1]
        qkv = t @ w_qkv
        q = rmsnorm(qkv[..., :nq].reshape(B, T, ATT_Q_HEADS, HEAD_DIM), q_gain)
        k = rmsnorm(qkv[..., nq:nq + nk].reshape(B, T, ATT_KV_HEADS, HEAD_DIM), k_gain)
        v = qkv[..., nq + nk:].reshape(B, T, ATT_KV_HEADS, HEAD_DIM)
        return q, k, v

    q, k, v = project(h)
    q, k = rope_2d(q, rows, cols), rope_2d(k, rows, cols)
    qc, kc, vc = project(hc)
    scale = HEAD_DIM ** -0.5
    sink_logit = sink.astype(F32).reshape(ATT_KV_HEADS, ATT_GROUP, 1, 1)

    nb = S // BLOCK
    qb = q.reshape(B, nb, BLOCK, ATT_KV_HEADS, ATT_GROUP, HEAD_DIM)

    def band(t):
        tp = jnp.pad(t, ((0, 0), (BLOCK, BLOCK), (0, 0), (0, 0))).reshape(B, nb + 2, BLOCK, ATT_KV_HEADS, HEAD_DIM)
        return jnp.concatenate([tp[:, :-2], tp[:, 1:-1], tp[:, 2:]], axis=2)

    kw, vw = band(k), band(v)
    s_win = jnp.einsum('bnqhgd,bnkhd->bnhgqk', qb, kw).astype(F32) * scale
    s_ctx = jnp.einsum('bnqhgd,blhd->bnhgql', qb, kc).astype(F32) * scale
    blk = jnp.arange(nb)[:, None, None] * BLOCK
    qpos = blk + jnp.arange(BLOCK)[None, :, None]
    kpos = blk - BLOCK + jnp.arange(3 * BLOCK)[None, None, :]
    mask = (jnp.abs(kpos - qpos) <= WINDOW) & (kpos >= 0) & (kpos < S)
    s_win = jnp.where(mask[None, :, None, None], s_win, NEG_INF)
    sinks = jnp.broadcast_to(sink_logit, s_win.shape[:-1] + (1,))
    p = jax.nn.softmax(jnp.concatenate([s_win, s_ctx, sinks], axis=-1), axis=-1)
    p_win = p[..., :3 * BLOCK].astype(v.dtype)
    p_ctx = p[..., 3 * BLOCK:3 * BLOCK + L].astype(v.dtype)
    o = jnp.einsum('bnhgqk,bnkhd->bnqhgd', p_win, vw) + jnp.einsum('bnhgql,blhd->bnqhgd', p_ctx, vc)
    y = o.reshape(B, S, D_MODEL) @ w_o
    if not ctx_out:
        return y, None

    qcg = qc.reshape(B, L, ATT_KV_HEADS, ATT_GROUP, HEAD_DIM)
    s_cc = jnp.einsum('blhgd,bmhd->bhglm', qcg, kc).astype(F32) * scale
    sinks_c = jnp.broadcast_to(sink_logit, s_cc.shape[:-1] + (1,))
    pc = jax.nn.softmax(jnp.concatenate([s_cc, sinks_c], axis=-1), axis=-1)[..., :L].astype(vc.dtype)
    yc = jnp.einsum('bhglm,bmhd->blhgd', pc, vc).reshape(B, L, D_MODEL) @ w_o
    return y, yc


def wkv_scan(r, w, k, v, a, b, s0, reverse):
    def step(s, inp):
        r_t, w_t, k_t, v_t, a_t, b_t = inp
        sa = jnp.einsum('bhvk,bhk->bhv', s, a_t)
        s = s * w_t[:, :, None, :] + sa[..., None] * b_t[:, :, None, :] + v_t[..., None] * k_t[:, :, None, :]
        return s, jnp.einsum('bhvk,bhk->bhv', s, r_t)

    xs = tuple(jnp.moveaxis(t, 1, 0) for t in (r, w, k, v, a, b))
    s_final, ys = lax.scan(step, s0, xs, reverse=reverse)
    return s_final, jnp.moveaxis(ys, 0, 1)


def rwkv_features(h, mu, w_rkv, w0, w1, w2, a0, a1, a2, k_k, k_a):
    B, T, _ = h.shape
    hn = (RWKV_HEADS, HEAD_DIM)
    hp = jnp.pad(h, ((0, 0), (1, 1), (0, 0)))
    xx = 0.5 * (hp[:, :-2] + hp[:, 2:]) - h
    xr, xw, xk, xv, xa, xg = [h + xx * mu[m] for m in range(6)]

    def heads(t):
        return t.astype(F32).reshape(B, T, *hn)

    r = heads(xr @ w_rkv[0])
    k = heads(xk @ w_rkv[1])
    v = heads(xv @ w_rkv[2])
    decay = tuple(
        heads(jnp.exp(-jnp.exp(-jax.nn.softplus(-(w0[d] + jnp.tanh(xw @ w1[d]) @ w2[d]).astype(F32)) - 0.5)))
        for d in range(2))
    a = heads(jax.nn.sigmoid((a0 + (xa @ a1) @ a2).astype(F32)))
    kk = k * k_k.astype(F32).reshape(hn)
    kk = kk / jnp.maximum(jnp.sqrt(jnp.sum(kk * kk, axis=-1, keepdims=True)), 1e-12)
    k = k * (1.0 + (a - 1.0) * k_a.astype(F32).reshape(hn))
    return r, k, v, -kk, kk * a, decay, xg


def group_norm(y, g, b):
    hn = (RWKV_HEADS, HEAD_DIM)
    mean = jnp.mean(y, axis=-1, keepdims=True)
    var = jnp.mean(jnp.square(y - mean), axis=-1, keepdims=True)
    return (y - mean) * lax.rsqrt(var + GN_EPS) * g.astype(F32).reshape(hn) + b.astype(F32).reshape(hn)


def rwkv_readout(y_fwd, y_bwd, r, k, v, xg, g1, g2, r_k, gn_g, gn_b, w_o):
    B, T = xg.shape[:2]
    bonus = jnp.sum(r * k * r_k.astype(F32), axis=-1, keepdims=True) * v

    def gate(d):
        return (jax.nn.sigmoid(xg @ g1[d]) @ g2[d]).astype(F32).reshape(B, T, RWKV_HEADS, HEAD_DIM)

    o = (group_norm(y_fwd, gn_g, gn_b) + bonus) * gate(0) + (group_norm(y_bwd, gn_g, gn_b) + bonus) * gate(1)
    return o.reshape(B, T, D_MODEL).astype(xg.dtype) @ w_o


def rwkv_mixer(h, hc, mu, w_rkv, w0, w1, w2, a0, a1, a2, g1, g2, k_k, k_a, r_k, gn_g, gn_b, w_o, ctx_out):
    feat = (mu, w_rkv, w0, w1, w2, a0, a1, a2, k_k, k_a)
    rc, kc, vc, nac, bc, decc, xgc = rwkv_features(hc, *feat)
    r, k, v, na, b, dec, xg = rwkv_features(h, *feat)
    zero = jnp.zeros((h.shape[0], RWKV_HEADS, HEAD_DIM, HEAD_DIM), F32)
    s_fwd, yc_fwd = wkv_scan(rc, decc[0], kc, vc, nac, bc, zero, False)
    s_bwd, yc_bwd = wkv_scan(rc, decc[1], kc, vc, nac, bc, zero, True)
    _, y_fwd = wkv_scan(r, dec[0], k, v, na, b, s_fwd, False)
    _, y_bwd = wkv_scan(r, dec[1], k, v, na, b, s_bwd, True)
    ro = (g1, g2, r_k, gn_g, gn_b, w_o)
    y = rwkv_readout(y_fwd, y_bwd, r, k, v, xg, *ro)
    if not ctx_out:
        return y, None
    return y, rwkv_readout(yc_fwd, yc_bwd, rc, kc, vc, xgc, *ro)


def setup_inputs(seed: int = 0) -> dict:
    key = jax.random.key(seed)
    ks = iter(jax.random.split(key, 40))
    D = D_MODEL

    def nrm(shape, s):
        return jax.random.normal(next(ks), shape, F32) * s

    def unif(shape, lo, hi):
        return jax.random.uniform(next(ks), shape, F32, lo, hi)

    qkv_cols = (ATT_Q_HEADS + 2 * ATT_KV_HEADS) * HEAD_DIM
    return {
        'x': nrm((BATCH, SEQ, D), 1.0),
        'c': nrm((BATCH, D), 1.0),
        'ctx': nrm((BATCH, CTX_LEN, D), 1.0),
        'c_ctx': nrm((D,), 1.0),
        'ada_w': nrm((DEPTH, D, 6 * D), 0.5 * D ** -0.5),
        'ada_b': nrm((DEPTH, 6 * D), 0.01),
        'norm1_g': 1.0 + nrm((DEPTH, D), 0.1),
        'norm2_g': 1.0 + nrm((DEPTH, D), 0.1),
        'mlp_w1': nrm((DEPTH, D, D_FF), D ** -0.5),
        'mlp_w2': nrm((DEPTH, D_FF, D), D_FF ** -0.5),
        'att_w_qkv': nrm((N_ATT, D, qkv_cols), D ** -0.5),
        'att_q_gain': 1.0 + nrm((N_ATT, HEAD_DIM), 0.1),
        'att_k_gain': 1.0 + nrm((N_ATT, HEAD_DIM), 0.1),
        'att_sink': nrm((N_ATT, ATT_Q_HEADS), 1.0),
        'att_w_o': nrm((N_ATT, D, D), D ** -0.5),
        'rwkv_mu': unif((N_RWKV, 6, D), 0.0, 1.0),
        'rwkv_w_rkv': nrm((N_RWKV, 3, D, D), D ** -0.5),
        'rwkv_w0': unif((N_RWKV, 2, D), -6.0, 1.0),
        'rwkv_w1': nrm((N_RWKV, 2, D, DECAY_LORA), D ** -0.5),
        'rwkv_w2': nrm((N_RWKV, 2, DECAY_LORA, D), 0.5 * DECAY_LORA ** -0.5),
        'rwkv_a0': nrm((N_RWKV, D), 0.5),
        'rwkv_a1': nrm((N_RWKV, D, ICLR_LORA), D ** -0.5),
        'rwkv_a2': nrm((N_RWKV, ICLR_LORA, D), ICLR_LORA ** -0.5),
        'rwkv_g1': nrm((N_RWKV, 2, D, GATE_LORA), D ** -0.5),
        'rwkv_g2': nrm((N_RWKV, 2, GATE_LORA, D), GATE_LORA ** -0.5),
        'rwkv_k_k': 0.85 + nrm((N_RWKV, D), 0.05),
        'rwkv_k_a': 1.0 + nrm((N_RWKV, D), 0.05),
        'rwkv_r_k': nrm((N_RWKV, RWKV_HEADS, HEAD_DIM), 0.1),
        'rwkv_gn_g': 1.0 + nrm((N_RWKV, D), 0.1),
        'rwkv_gn_b': nrm((N_RWKV, D), 0.01),
        'rwkv_w_o': nrm((N_RWKV, D, D), D ** -0.5),
    }


def reference(x, c, ctx, c_ctx, ada_w, ada_b, norm1_g, norm2_g, mlp_w1, mlp_w2,
              att_w_qkv, att_q_gain, att_k_gain, att_sink, att_w_o,
              rwkv_mu, rwkv_w_rkv, rwkv_w0, rwkv_w1, rwkv_w2, rwkv_a0, rwkv_a1, rwkv_a2,
              rwkv_g1, rwkv_g2, rwkv_k_k, rwkv_k_a, rwkv_r_k, rwkv_gn_g, rwkv_gn_b, rwkv_w_o):
    S = x.shape[1]
    n_rows = S // GRID_W
    rows = jnp.repeat(jnp.arange(n_rows), GRID_W)
    cols = jnp.tile(jnp.arange(GRID_W), n_rows)
    silu_c = jax.nn.silu(c)[:, None, :]
    silu_cc = jax.nn.silu(c_ctx)[None, None, :]
    xc = ctx
    for i in range(DEPTH):
        ctx_out = i < DEPTH - 1
        j = i // N_MIXERS
        sh1, sc1, gt1, sh2, sc2, gt2 = jnp.split(silu_c @ ada_w[i] + ada_b[i], 6, axis=-1)
        csh1, csc1, cgt1, csh2, csc2, cgt2 = jnp.split(silu_cc @ ada_w[i] + ada_b[i], 6, axis=-1)
        h = rmsnorm(x, norm1_g[i]) * (1.0 + sc1) + sh1
        hc = rmsnorm(xc, norm1_g[i]) * (1.0 + csc1) + csh1
        if i % N_MIXERS == 0:
            y, yc = attention_mixer(h, hc, att_w_qkv[j], att_q_gain[j], att_k_gain[j], att_sink[j],
                                    att_w_o[j], rows, cols, ctx_out)
        else:
            y, yc = rwkv_mixer(h, hc, rwkv_mu[j], rwkv_w_rkv[j], rwkv_w0[j], rwkv_w1[j], rwkv_w2[j],
                               rwkv_a0[j], rwkv_a1[j], rwkv_a2[j], rwkv_g1[j], rwkv_g2[j],
                               rwkv_k_k[j], rwkv_k_a[j], rwkv_r_k[j], rwkv_gn_g[j], rwkv_gn_b[j],
                               rwkv_w_o[j], ctx_out)
        x = x + gt1 * y
        x = x + gt2 * sq_relu_mlp(rmsnorm(x, norm2_g[i]) * (1.0 + sc2) + sh2, mlp_w1[i], mlp_w2[i])
        if ctx_out:
            xc = xc + cgt1 * yc
            xc = xc + cgt2 * sq_relu_mlp(rmsnorm(xc, norm2_g[i]) * (1.0 + csc2) + csh2, mlp_w1[i], mlp_w2[i])
    return x
```

```python
import functools
import math

import jax
import jax.numpy as jnp
from jax import lax
from jax.experimental import pallas as pl
from jax.experimental.pallas import tpu as pltpu

F32 = jnp.float32
BF16 = jnp.bfloat16

HEAD_DIM = 64
LANES = 128
HALO = 8
ATT_GROUP = 4
ATT_BLOCK = 128
ROPE_BASE = 10000.0
GRID_W = 64
NORM_EPS = 1e-6
GN_EPS = 64e-5
NEG_INF = -1e30
SCAN_CHUNK = 64
ROW_TILE = 256
FF_CHUNK = 1024
VMEM_LIMIT = 56 * 1024 * 1024


def _params(sem):
    return pltpu.CompilerParams(dimension_semantics=sem, vmem_limit_bytes=VMEM_LIMIT)


def _const_spec(shape):
    n = len(shape)
    return pl.BlockSpec(shape, lambda *_: (0,) * n, pipeline_mode=pl.Buffered(1))


def _rms(x):
    return x * lax.rsqrt(jnp.mean(x * x, axis=-1, keepdims=True) + NORM_EPS)


def _bdot(a, b):
    return jnp.dot(a.astype(BF16), b.astype(BF16), preferred_element_type=F32)


def _bdot_nt(a, b):
    return lax.dot_general(a.astype(BF16), b.astype(BF16), (((1,), (1,)), ((), ())),
                           preferred_element_type=F32)


def _group_sum_block(xb, gmat):
    hi = xb.astype(BF16)
    lo = (xb - hi.astype(F32)).astype(BF16)
    return jnp.dot(jnp.concatenate([hi, lo], axis=1), gmat, preferred_element_type=F32)


def _group_sum(x, gmat):
    n = x.shape[1] // LANES
    return jnp.concatenate(
        [_group_sum_block(x[:, j * LANES:(j + 1) * LANES], gmat) for j in range(n)], axis=1)


def _group_matrix():
    i = jnp.arange(2 * LANES)[:, None]
    j = jnp.arange(LANES)[None, :]
    return (((i % LANES) // HEAD_DIM) == (j // HEAD_DIM)).astype(BF16)


def _ada_kernel(c_ref, w_ref, b_ref, o_ref):
    c = c_ref[...]
    s = c * jax.nn.sigmoid(c)
    o_ref[...] = jnp.dot(s, w_ref[...], precision=lax.Precision.HIGHEST,
                         preferred_element_type=F32) + b_ref[...]


def _ada_mods(cc, ada_w, ada_b):
    depth, d, n = ada_w.shape
    tn = n // 4
    return pl.pallas_call(
        _ada_kernel,
        grid=(depth, n // tn),
        in_specs=[
            pl.BlockSpec((HALO, d), lambda i, j: (0, 0)),
            pl.BlockSpec((None, d, tn), lambda i, j: (i, 0, j)),
            pl.BlockSpec((None, 1, tn), lambda i, j: (i, 0, j)),
        ],
        out_specs=pl.BlockSpec((None, HALO, tn), lambda i, j: (i, 0, j)),
        out_shape=jax.ShapeDtypeStruct((depth, HALO, n), F32),
        compiler_params=_params(("parallel", "parallel")),
        name="ada_mods",
    )(cc, ada_w, ada_b.reshape(depth, 1, n))


def _qkv_kernel(ctx_ref, x_ref, sh_ref, sc_ref, g_ref, w_ref, qg_ref, kg_ref, cos_ref, sin_ref,
                gmat_ref, q_ref, kt_ref, v_ref, *, n_q, n_kv):
    t = pl.program_id(1)
    xin = jnp.where(t == 0, ctx_ref[...], x_ref[...])
    h = _rms(xin) * g_ref[...] * (1.0 + sc_ref[...]) + sh_ref[...]
    qkv = jnp.dot(h.astype(BF16), w_ref[...], preferred_element_type=F32)
    gmat = gmat_ref[...]
    cos = cos_ref[...]
    sin = sin_ref[...]
    lane = lax.broadcasted_iota(jnp.int32, cos.shape, 1)
    first_half = (lane % 32) < 16
    low = lane < HEAD_DIM
    nqb = n_q // LANES
    nkb = n_kv // LANES
    for j in range(nqb + nkb):
        xb = qkv[:, j * LANES:(j + 1) * LANES]
        ms = _group_sum_block(xb * xb, gmat) * (1.0 / HEAD_DIM)
        gain = qg_ref[...] if j < nqb else kg_ref[...]
        xn = xb * lax.rsqrt(ms + NORM_EPS) * gain
        partner = jnp.where(first_half, pltpu.roll(xn, LANES - 16, 1), pltpu.roll(xn, 16, 1))
        xr = xn * cos + partner * sin
        if j < nqb:
            q_ref[:, j * LANES:(j + 1) * LANES] = (xr * (HEAD_DIM ** -0.5)).astype(BF16)
        else:
            kt = xr.T
            jj = j - nqb
            ke = kt[:HEAD_DIM]
            ko = kt[HEAD_DIM:]
            kt_ref[2 * jj] = jnp.concatenate([ke, ke], axis=0).astype(BF16)
            kt_ref[2 * jj + 1] = jnp.concatenate([ko, ko], axis=0).astype(BF16)
    for j in range(nkb):
        vb = qkv[:, n_q + n_kv + j * LANES:n_q + n_kv + (j + 1) * LANES]
        sw = pltpu.roll(vb, HEAD_DIM, 1)
        v_ref[2 * j] = jnp.where(low, vb, sw).astype(BF16)
        v_ref[2 * j + 1] = jnp.where(low, sw, vb).astype(BF16)


def _mod_spec(layer, slot, n_batch, ctx_tiles, d):
    base = (layer * 6 + slot) * HALO

    def idx(b, t):
        return (base + jnp.where(t < ctx_tiles, n_batch, b), 0, 0)

    return pl.BlockSpec((None, 1, d), idx)


def _qkv_call(x, ctx, modsr, g, w_qkv, qg, kg, cos, sin, gmat, n_q, n_kv):
    bsz, s, d = x.shape
    l = ctx.shape[1]
    tm = ROW_TILE
    assert l == tm and s % tm == 0
    nt = (l + s) // tm
    t_all = l + s
    nkvh = n_kv // HEAD_DIM
    return pl.pallas_call(
        functools.partial(_qkv_kernel, n_q=n_q, n_kv=n_kv),
        grid=(bsz, nt),
        in_specs=[
            pl.BlockSpec((None, tm, d), lambda b, t: (b, 0, 0)),
            pl.BlockSpec((None, tm, d), lambda b, t: (b, jnp.maximum(t - 1, 0), 0)),
            _mod_spec(0, 0, bsz, 1, d),
            _mod_spec(0, 1, bsz, 1, d),
            _const_spec((1, d)),
            _const_spec(w_qkv.shape),
            _const_spec((1, LANES)),
            _const_spec((1, LANES)),
            pl.BlockSpec((tm, LANES), lambda b, t: (t, 0)),
            pl.BlockSpec((tm, LANES), lambda b, t: (t, 0)),
            _const_spec(gmat.shape),
        ],
        out_specs=[
            pl.BlockSpec((None, tm, n_q), lambda b, t: (b, t, 0)),
            pl.BlockSpec((None, nkvh, LANES, tm), lambda b, t: (b, 0, 0, t)),
            pl.BlockSpec((None, nkvh, tm, LANES), lambda b, t: (b, 0, t, 0)),
        ],
        out_shape=[
            jax.ShapeDtypeStruct((bsz, t_all, n_q), BF16),
            jax.ShapeDtypeStruct((bsz, nkvh, LANES, t_all), BF16),
            jax.ShapeDtypeStruct((bsz, nkvh, t_all, LANES), BF16),
        ],
        compiler_params=_params(("parallel", "arbitrary")),
        name="qkv_rope",
    )(ctx, x, modsr, modsr, g, w_qkv, qg, kg, cos, sin, gmat)


def _attn_kernel(sink_ref, q_ref, kc_ref, k0_ref, k1_ref, k2_ref, vc_ref, v0_ref, v1_ref, v2_ref,
                 o_ref, *, n_blocks, ctx_blocks, n_kv_heads):
    t = pl.program_id(1)
    bq = ATT_BLOCK
    m_rows = ATT_GROUP * bq
    row = lax.broadcasted_iota(jnp.int32, (m_rows, bq), 0) % bq
    col = lax.broadcasted_iota(jnp.int32, (m_rows, bq), 1)
    far = 4 * bq
    latent = t >= ctx_blocks
    keep0 = (col - row) >= jnp.where(t >= ctx_blocks + 1, 0, far)
    keep1 = (col - row) >= jnp.where(latent, -far, far)
    keep2 = (row - col) >= jnp.where(jnp.logical_and(latent, t <= n_blocks - 2), 0, far)
    lane = lax.broadcasted_iota(jnp.int32, (bq, LANES), 1)
    low = lane < HEAD_DIM
    for h in range(n_kv_heads):
        parts = []
        for g in range(ATT_GROUP):
            blk = 2 * h + g // 2
            qp = q_ref[:, blk * LANES:(blk + 1) * LANES]
            keep = low if g % 2 == 0 else jnp.logical_not(low)
            parts.append(jnp.where(keep, qp, jnp.zeros_like(qp)))
        qs = jnp.concatenate(parts, axis=0)
        sc = jnp.dot(qs, kc_ref[h], preferred_element_type=F32)
        s0 = jnp.where(keep0, jnp.dot(qs, k0_ref[h], preferred_element_type=F32), NEG_INF)
        s1 = jnp.where(keep1, jnp.dot(qs, k1_ref[h], preferred_element_type=F32), NEG_INF)
        s2 = jnp.where(keep2, jnp.dot(qs, k2_ref[h], preferred_element_type=F32), NEG_INF)
        sink = jnp.concatenate(
            [jnp.full((bq, 1), sink_ref[h * ATT_GROUP + g], F32) for g in range(ATT_GROUP)], axis=0)
        m = jnp.maximum(jnp.max(sc, axis=-1, keepdims=True), sink)
        for s_ in (s0, s1, s2):
            m = jnp.maximum(m, jnp.max(s_, axis=-1, keepdims=True))
        pc = jnp.exp(sc - m)
        p0 = jnp.exp(s0 - m)
        p1 = jnp.exp(s1 - m)
        p2 = jnp.exp(s2 - m)
        den = (jnp.sum(pc, axis=-1, keepdims=True) + jnp.sum(p0, axis=-1, keepdims=True)
               + jnp.sum(p1, axis=-1, keepdims=True) + jnp.sum(p2, axis=-1, keepdims=True)
               + jnp.exp(sink - m))
        o = (jnp.dot(pc.astype(BF16), vc_ref[h], preferred_element_type=F32)
             + jnp.dot(p0.astype(BF16), v0_ref[h], preferred_element_type=F32)
             + jnp.dot(p1.astype(BF16), v1_ref[h], preferred_element_type=F32)
             + jnp.dot(p2.astype(BF16), v2_ref[h], preferred_element_type=F32))
        o = o / den
        for jj in range(2):
            oe = o[(2 * jj) * bq:(2 * jj + 1) * bq]
            oo = o[(2 * jj + 1) * bq:(2 * jj + 2) * bq]
            blk = 2 * h + jj
            o_ref[:, blk * LANES:(blk + 1) * LANES] = jnp.where(low, oe, oo).astype(BF16)


def _attn_call(sink, q, kt, v, ctx_len):
    bsz, t_all, n_q = q.shape
    nkvh = kt.shape[1]
    bq = ATT_BLOCK
    nb = t_all // bq
    cb = ctx_len // bq
    lo = lambda t: jnp.maximum(t - 1, 0)
    hi = lambda t: jnp.minimum(t + 1, nb - 1)
    kspec = lambda f: pl.BlockSpec((None, nkvh, LANES, bq), lambda b, t: (b, 0, 0, f(t)))
    vspec = lambda f: pl.BlockSpec((None, nkvh, bq, LANES), lambda b, t: (b, 0, f(t), 0))
    return pl.pallas_call(
        functools.partial(_attn_kernel, n_blocks=nb, ctx_blocks=cb, n_kv_heads=nkvh),
        grid=(bsz, nb),
        in_specs=[
            pl.BlockSpec(memory_space=pltpu.SMEM),
            pl.BlockSpec((None, bq, n_q), lambda b, t: (b, t, 0)),
            pl.BlockSpec((None, nkvh, LANES, ctx_len), lambda b, t: (b, 0, 0, 0)),
            kspec(lo), kspec(lambda t: t), kspec(hi),
            pl.BlockSpec((None, nkvh, ctx_len, LANES), lambda b, t: (b, 0, 0, 0)),
            vspec(lo), vspec(lambda t: t), vspec(hi),
        ],
        out_specs=pl.BlockSpec((None, bq, n_q), lambda b, t: (b, t, 0)),
        out_shape=jax.ShapeDtypeStruct((bsz, t_all, n_q), BF16),
        compiler_params=_params(("parallel", "arbitrary")),
        name="window_attention",
    )(sink, q, kt, kt, kt, kt, v, v, v, v)


def _mlp_tail(x1, sh2, sc2, gt2, g2, w1_ref, w2_ref):
    h2 = (_rms(x1) * g2 * (1.0 + sc2) + sh2).astype(BF16)
    acc = jnp.zeros(x1.shape, F32)
    for c in range(w1_ref.shape[1] // FF_CHUNK):
        hid = jnp.dot(h2, w1_ref[:, c * FF_CHUNK:(c + 1) * FF_CHUNK], preferred_element_type=F32)
        hid = jnp.square(jnp.maximum(hid, 0.0)).astype(BF16)
        acc = acc + jnp.dot(hid, w2_ref[c * FF_CHUNK:(c + 1) * FF_CHUNK, :],
                            preferred_element_type=F32)
    return x1 + gt2 * acc


def _attn_out_mlp_kernel(o_ref, ctx_ref, x_ref, gt1_ref, sh2_ref, sc2_ref, gt2_ref, g2_ref,
                         wo_ref, w1_ref, w2_ref, out_ref):
    t = pl.program_id(1)
    x0 = jnp.where(t == 0, ctx_ref[...], x_ref[...])
    y = jnp.dot(o_ref[...], wo_ref[...], preferred_element_type=F32)
    x1 = x0 + gt1_ref[...] * y
    out_ref[...] = _mlp_tail(x1, sh2_ref[...], sc2_ref[...], gt2_ref[...], g2_ref[...],
                             w1_ref, w2_ref)


def _attn_out_mlp_call(o, x, ctx, modsr, g2, w_o, w1, w2):
    bsz, s, d = x.shape
    l = ctx.shape[1]
    tm = ROW_TILE
    nt = (l + s) // tm
    return pl.pallas_call(
        _attn_out_mlp_kernel,
        grid=(bsz, nt),
        in_specs=[
            pl.BlockSpec((None, tm, d), lambda b, t: (b, t, 0)),
            pl.BlockSpec((None, tm, d), lambda b, t: (b, 0, 0)),
            pl.BlockSpec((None, tm, d), lambda b, t: (b, jnp.maximum(t - 1, 0), 0)),
            _mod_spec(0, 2, bsz, 1, d),
            _mod_spec(0, 3, bsz, 1, d),
            _mod_spec(0, 4, bsz, 1, d),
            _mod_spec(0, 5, bsz, 1, d),
            _const_spec((1, d)),
            _const_spec(w_o.shape),
            _const_spec(w1.shape),
            _const_spec(w2.shape),
        ],
        out_specs=pl.BlockSpec((None, tm, d), lambda b, t: (b, t, 0)),
        out_shape=jax.ShapeDtypeStruct((bsz, l + s, d), F32),
        compiler_params=_params(("parallel", "arbitrary")),
        name="attn_out_mlp",
    )(o, ctx, x, modsr, modsr, modsr, modsr, g2, w_o, w1, w2)


def _rwkv_feat_kernel(x_ref, xp_ref, xn_ref, sh_ref, sc_ref, g_ref, mu_ref, wrkv_ref, w0_ref,
                      w1_ref, w2_ref, a0_ref, a1_ref, a2_ref, g1_ref, g2_ref, kk_ref, ka_ref,
                      rk_ref, gmat_ref,
                      r_out, k_out, v_out, a_out, b_out, lw0_out, lw1_out, bonus_out, gate0_out,
                      gate1_out, *, n_tiles, ctx_tiles):
    t = pl.program_id(1)
    d = x_ref.shape[1]
    tm = x_ref.shape[0]
    g = g_ref[...]
    scale = 1.0 + sc_ref[...]
    shift = sh_ref[...]

    def modulate(xv):
        return _rms(xv) * g * scale + shift

    h = modulate(x_ref[...])
    first = jnp.logical_or(t == 0, t == ctx_tiles)
    last = jnp.logical_or(t == ctx_tiles - 1, t == n_tiles - 1)
    prev_row = jnp.where(first, 0.0, modulate(xp_ref[...])[HALO - 1:HALO])
    next_row = jnp.where(last, 0.0, modulate(xn_ref[...])[0:1])
    rows = lax.broadcasted_iota(jnp.int32, (tm, d), 0)
    h_prev = jnp.where(rows == 0, prev_row, pltpu.roll(h, 1, 0))
    h_next = jnp.where(rows == tm - 1, next_row, pltpu.roll(h, tm - 1, 0))
    xx = 0.5 * (h_prev + h_next) - h

    def mix(m):
        return (h + xx * mu_ref[m:m + 1, :]).astype(BF16)

    gmat = gmat_ref[...]
    r = jnp.dot(mix(0), wrkv_ref[0], preferred_element_type=F32)
    k = jnp.dot(mix(2), wrkv_ref[1], preferred_element_type=F32)
    v = jnp.dot(mix(3), wrkv_ref[2], preferred_element_type=F32)
    dw = jnp.tanh(jnp.dot(mix(1), w1_ref[...], preferred_element_type=F32))
    z = w0_ref[...] + jnp.dot(dw.astype(BF16), w2_ref[...], preferred_element_type=F32)
    lw = -jax.nn.sigmoid(z) * math.exp(-0.5)
    lw0_out[...] = lw[:, :d]
    lw1_out[...] = lw[:, d:]
    al = jnp.dot(mix(4), a1_ref[...], preferred_element_type=F32)
    alr = jax.nn.sigmoid(a0_ref[...] + jnp.dot(al.astype(BF16), a2_ref[...],
                                               preferred_element_type=F32))
    kk = k * kk_ref[...]
    nrm = jnp.sqrt(_group_sum(kk * kk, gmat))
    kk = kk / jnp.maximum(nrm, 1e-12)
    k2 = k * (1.0 + (alr - 1.0) * ka_ref[...])
    r_out[...] = r
    k_out[...] = k2
    v_out[...] = v
    a_out[...] = -kk
    b_out[...] = kk * alr
    bonus_out[...] = _group_sum(r * k2 * rk_ref[...], gmat) * v
    gg = jax.nn.sigmoid(jnp.dot(mix(5), g1_ref[...], preferred_element_type=F32))
    gate = jnp.dot(gg.astype(BF16), g2_ref[...], preferred_element_type=F32)
    gate0_out[...] = gate[:, :d]
    gate1_out[...] = gate[:, d:]


def _rwkv_feat_call(xu, modsr, n_batch, ctx_len, g, mu, wrkv, w0c, w1c, w2bd, a0, a1, a2, g1c, g2bd,
                    k_k, k_a, r_k, gmat):
    bsz, t_all, d = xu.shape
    tm = ROW_TILE
    nt = t_all // tm
    ct = ctx_len // tm
    hb = tm // HALO
    tile = pl.BlockSpec((None, tm, d), lambda b, t: (b, t, 0))
    consts = [g, mu, wrkv, w0c, w1c, w2bd, a0, a1, a2, g1c, g2bd, k_k, k_a, r_k, gmat]
    return pl.pallas_call(
        functools.partial(_rwkv_feat_kernel, n_tiles=nt, ctx_tiles=ct),
        grid=(bsz, nt),
        in_specs=[
            tile,
            pl.BlockSpec((None, HALO, d), lambda b, t: (b, jnp.maximum(t * hb - 1, 0), 0)),
            pl.BlockSpec((None, HALO, d),
                         lambda b, t: (b, jnp.minimum((t + 1) * hb, t_all // HALO - 1), 0)),
            _mod_spec(1, 0, n_batch, ct, d),
            _mod_spec(1, 1, n_batch, ct, d),
        ] + [_const_spec(c.shape) for c in consts],
        out_specs=[tile] * 10,
        out_shape=[jax.ShapeDtypeStruct((bsz, t_all, d), F32)] * 10,
        compiler_params=_params(("parallel", "arbitrary")),
        name="rwkv_features",
    )(xu, xu, xu, modsr, modsr, *consts)


def _split3(x):
    h1 = x.astype(BF16)
    r1 = x - h1.astype(F32)
    h2 = r1.astype(BF16)
    h3 = (r1 - h2.astype(F32)).astype(BF16)
    return h1, h2, h3


def _scan_kernel(r_ref, k_ref, v_ref, a_ref, b_ref, lw_ref, y_ref, s_ref, *, reverse):
    @pl.when(pl.program_id(1) == 0)
    def _():
        s_ref[...] = jnp.zeros(s_ref.shape, F32)

    c, d = r_ref.shape
    n_heads = d // HEAD_DIM
    ii = lax.broadcasted_iota(jnp.int32, (c, c), 0)
    jj = lax.broadcasted_iota(jnp.int32, (c, c), 1)
    if reverse:
        incl, strict = ii <= jj, ii < jj
    else:
        incl, strict = ii >= jj, ii > jj
    tri = jnp.where(incl, 1.0, 0.0).astype(BF16)
    eye = jnp.where(ii == jj, 1.0, 0.0)
    lw = lw_ref[...]
    cs = sum(jnp.dot(tri, p, preferred_element_type=F32) for p in _split3(lw))
    tot = cs[0:1] if reverse else cs[c - 1:c]
    e_pos = jnp.exp(cs)
    e_neg = jnp.exp(-cs)
    e_rem = jnp.exp(tot - cs)
    e_tot = jnp.exp(tot)
    rv, kv, vv, av, bv = r_ref[...], k_ref[...], v_ref[...], a_ref[...], b_ref[...]
    a_t = av * jnp.exp(cs - lw)
    r_t = rv * e_pos
    b_t = bv * e_neg
    k_t = kv * e_neg
    b_p = bv * e_rem
    k_p = kv * e_rem
    for h in range(n_heads):
        sl = slice(h * HEAD_DIM, (h + 1) * HEAD_DIM)
        ah, rh, bh, kh, vh = a_t[:, sl], r_t[:, sl], b_t[:, sl], k_t[:, sl], vv[:, sl]
        s0 = s_ref[h]
        a_ab = jnp.where(strict, _bdot_nt(ah, bh), 0.0)
        a_ak = jnp.where(strict, _bdot_nt(ah, kh), 0.0)
        a_rb = jnp.where(incl, _bdot_nt(rh, bh), 0.0)
        a_rk = jnp.where(incl, _bdot_nt(rh, kh), 0.0)
        rhs = _bdot_nt(ah, s0) + _bdot(a_ak, vh)
        npow = a_ab
        inv = eye + npow
        span = 2
        while span < c:
            npow = _bdot(npow, npow)
            inv = inv + _bdot(inv, npow)
            span *= 2
        u = _bdot(inv, rhs)
        y_ref[:, sl] = _bdot_nt(rh, s0) + _bdot(a_rb, u) + _bdot(a_rk, vh)
        s_ref[h] = s0 * e_tot[:, sl] + _bdot(u.T, b_p[:, sl]) + _bdot(vh.T, k_p[:, sl])


def _scan_call(r, k, v, a, b, lw, ctx_len, reverse):
    bsz, t_all, d = r.shape
    c = SCAN_CHUNK
    nc = t_all // c
    cc = ctx_len // c
    if reverse:
        order = lambda i: jnp.where(i < cc, cc - 1 - i, nc - 1 - (i - cc))
    else:
        order = lambda i: i
    spec = pl.BlockSpec((None, c, d), lambda bb, i: (bb, order(i), 0))
    return pl.pallas_call(
        functools.partial(_scan_kernel, reverse=reverse),
        grid=(bsz, nc),
        in_specs=[spec] * 6,
        out_specs=spec,
        out_shape=jax.ShapeDtypeStruct((bsz, t_all, d), F32),
        scratch_shapes=[pltpu.VMEM((d // HEAD_DIM, HEAD_DIM, HEAD_DIM), F32)],
        compiler_params=_params(("parallel", "arbitrary")),
        name="wkv_scan_bwd" if reverse else "wkv_scan_fwd",
    )(r, k, v, a, b, lw)


def _group_norm(y, gmat, g, b):
    mean = _group_sum(y, gmat) * (1.0 / HEAD_DIM)
    yc = y - mean
    var = _group_sum(yc * yc, gmat) * (1.0 / HEAD_DIM)
    return yc * lax.rsqrt(var + GN_EPS) * g + b


def _rwkv_out_mlp_kernel(yf_ref, yb_ref, bonus_ref, g0_ref, g1_ref, x_ref, gt1_ref, sh2_ref,
                         sc2_ref, gt2_ref, gng_ref, gnb_ref, g2_ref, gmat_ref, wo_ref, w1_ref,
                         w2_ref, out_ref):
    gmat = gmat_ref[...]
    gng, gnb = gng_ref[...], gnb_ref[...]
    bonus = bonus_ref[...]
    o = ((_group_norm(yf_ref[...], gmat, gng, gnb) + bonus) * g0_ref[...]
         + (_group_norm(yb_ref[...], gmat, gng, gnb) + bonus) * g1_ref[...])
    y = jnp.dot(o.astype(BF16), wo_ref[...], preferred_element_type=F32)
    x1 = x_ref[...] + gt1_ref[...] * y
    out_ref[...] = _mlp_tail(x1, sh2_ref[...], sc2_ref[...], gt2_ref[...], g2_ref[...],
                             w1_ref, w2_ref)


def _rwkv_out_mlp_call(yf, yb, bonus, g0, g1, xu, modsr, ctx_len, gng, gnb, g2, gmat, w_o, w1, w2):
    bsz, t_all, d = xu.shape
    tm = ROW_TILE
    ct = ctx_len // tm
    ns = (t_all - ctx_len) // tm
    tile = pl.BlockSpec((None, tm, d), lambda b, t: (b, t + ct, 0))
    lat_mod = lambda slot: pl.BlockSpec((None, 1, d), lambda b, t: ((6 + slot) * HALO + b, 0, 0))
    consts = [gng, gnb, g2, gmat, w_o, w1, w2]
    return pl.pallas_call(
        _rwkv_out_mlp_kernel,
        grid=(bsz, ns),
        in_specs=[tile] * 6 + [lat_mod(2), lat_mod(3), lat_mod(4), lat_mod(5)]
        + [_const_spec(c.shape) for c in consts],
        out_specs=pl.BlockSpec((None, tm, d), lambda b, t: (b, t, 0)),
        out_shape=jax.ShapeDtypeStruct((bsz, t_all - ctx_len, d), F32),
        compiler_params=_params(("parallel", "arbitrary")),
        name="rwkv_out_mlp",
    )(yf, yb, bonus, g0, g1, xu, modsr, modsr, modsr, modsr, *consts)


def _rope_tables(ctx_len, s):
    half = HEAD_DIM // 4
    lane = jnp.arange(HEAD_DIM)
    freqs = ROPE_BASE ** (-(lane % half).astype(F32) / half)
    pos = jnp.arange(s)
    coord = jnp.where(lane[None, :] < HEAD_DIM // 2, (pos // GRID_W)[:, None], (pos % GRID_W)[:, None])
    ang = coord.astype(F32) * freqs[None, :]
    sign = jnp.where((lane % (2 * half)) < half, -1.0, 1.0)
    cos = jnp.concatenate([jnp.ones((ctx_len, HEAD_DIM), F32), jnp.cos(ang)], axis=0)
    sin = jnp.concatenate([jnp.zeros((ctx_len, HEAD_DIM), F32), jnp.sin(ang) * sign], axis=0)
    return jnp.tile(cos, (1, 2)), jnp.tile(sin, (1, 2))


def _block_diag2(w):
    z = jnp.zeros_like(w[0])
    return jnp.concatenate([jnp.concatenate([w[0], z], axis=1),
                            jnp.concatenate([z, w[1]], axis=1)], axis=0)


def kernel(x, c, ctx, c_ctx, ada_w, ada_b, norm1_g, norm2_g, mlp_w1, mlp_w2, att_w_qkv, att_q_gain, att_k_gain, att_sink, att_w_o, rwkv_mu, rwkv_w_rkv, rwkv_w0, rwkv_w1, rwkv_w2, rwkv_a0, rwkv_a1, rwkv_a2, rwkv_g1, rwkv_g2, rwkv_k_k, rwkv_k_a, rwkv_r_k, rwkv_gn_g, rwkv_gn_b, rwkv_w_o):
    bsz, s, d = x.shape
    l = ctx.shape[1]
    depth = ada_w.shape[0]
    assert depth == 2 and bsz < HALO and d % LANES == 0
    n_q = att_w_o.shape[1]
    n_kv = (att_w_qkv.shape[2] - n_q) // 2
    gmat = _group_matrix()

    cc = jnp.concatenate([c, c_ctx[None, :], jnp.zeros((HALO - bsz - 1, d), F32)], axis=0)
    mods = _ada_mods(cc, ada_w, ada_b)
    modsr = mods.reshape(depth, HALO, 6, d).transpose(0, 2, 1, 3).reshape(depth * 6 * HALO, 1, d)

    cos, sin = _rope_tables(l, s)
    q, kt, v = _qkv_call(x, ctx, modsr, norm1_g[0][None], att_w_qkv[0].astype(BF16),
                         jnp.tile(att_q_gain[0], 2)[None], jnp.tile(att_k_gain[0], 2)[None],
                         cos, sin, gmat, n_q, n_kv)
    o = _attn_call(att_sink[0], q, kt, v, l)
    xu = _attn_out_mlp_call(o, x, ctx, modsr, norm2_g[0][None], att_w_o[0].astype(BF16),
                            mlp_w1[0].astype(BF16), mlp_w2[0].astype(BF16))

    row = lambda a: a.reshape(1, -1)
    w1c = jnp.concatenate([rwkv_w1[0, 0], rwkv_w1[0, 1]], axis=1).astype(BF16)
    g1c = jnp.concatenate([rwkv_g1[0, 0], rwkv_g1[0, 1]], axis=1).astype(BF16)
    feats = _rwkv_feat_call(
        xu, modsr, bsz, l, norm1_g[1][None], rwkv_mu[0], rwkv_w_rkv[0].astype(BF16),
        row(rwkv_w0[0]), w1c, _block_diag2(rwkv_w2[0]).astype(BF16), row(rwkv_a0[0]),
        rwkv_a1[0].astype(BF16), rwkv_a2[0].astype(BF16), g1c,
        _block_diag2(rwkv_g2[0]).astype(BF16), row(rwkv_k_k[0]), row(rwkv_k_a[0]),
        row(rwkv_r_k[0]), gmat)
    r, k, vv, a, b, lw0, lw1, bonus, gate0, gate1 = feats
    y_fwd = _scan_call(r, k, vv, a, b, lw0, l, False)
    y_bwd = _scan_call(r, k, vv, a, b, lw1, l, True)
    return _rwkv_out_mlp_call(y_fwd, y_bwd, bonus, gate0, gate1, xu, modsr, l, row(rwkv_gn_g[0]),
                              row(rwkv_gn_b[0]), norm2_g[1][None], gmat, rwkv_w_o[0].astype(BF16),
                              mlp_w1[1].astype(BF16), mlp_w2[1].astype(BF16))
```

```python
import functools
import math

import jax
import jax.numpy as jnp
from jax import lax
from jax.experimental import pallas as pl
from jax.experimental.pallas import tpu as pltpu

F32 = jnp.float32
BF16 = jnp.bfloat16

HEAD_DIM = 64
LANES = 128
HALO = 8
ATT_GROUP = 4
ATT_BLOCK = 128
ROPE_BASE = 10000.0
GRID_W = 64
NORM_EPS = 1e-6
GN_EPS = 64e-5
NEG_INF = -1e30
SCAN_CHUNK = 64
SCAN_GROUP = 4
ROW_TILE = 256
FF_CHUNK = 1024
VMEM_LIMIT = 56 * 1024 * 1024


def _params(sem):
    return pltpu.CompilerParams(dimension_semantics=sem, vmem_limit_bytes=VMEM_LIMIT)


def _const_spec(shape):
    n = len(shape)
    return pl.BlockSpec(shape, lambda *_: (0,) * n, pipeline_mode=pl.Buffered(1))


def _rms(x):
    return x * lax.rsqrt(jnp.mean(x * x, axis=-1, keepdims=True) + NORM_EPS)


def _bdot(a, b):
    return jnp.dot(a.astype(BF16), b.astype(BF16), preferred_element_type=F32)


def _bdot_nt(a, b):
    return lax.dot_general(a.astype(BF16), b.astype(BF16), (((1,), (1,)), ((), ())),
                           preferred_element_type=F32)


def _group_sum_block(xb, gmat):
    hi = xb.astype(BF16)
    lo = (xb - hi.astype(F32)).astype(BF16)
    return jnp.dot(jnp.concatenate([hi, lo], axis=1), gmat, preferred_element_type=F32)


def _group_sum(x, gmat):
    n = x.shape[1] // LANES
    return jnp.concatenate(
        [_group_sum_block(x[:, j * LANES:(j + 1) * LANES], gmat) for j in range(n)], axis=1)


def _group_matrix():
    i = jnp.arange(2 * LANES)[:, None]
    j = jnp.arange(LANES)[None, :]
    return (((i % LANES) // HEAD_DIM) == (j // HEAD_DIM)).astype(BF16)


def _ada_kernel(c_ref, w_ref, b_ref, o_ref):
    c = c_ref[...]
    s = c * jax.nn.sigmoid(c)
    o_ref[...] = jnp.dot(s, w_ref[...], precision=lax.Precision.HIGHEST,
                         preferred_element_type=F32) + b_ref[...]


def _ada_mods(cc, ada_w, ada_b):
    depth, d, n = ada_w.shape
    tn = n // 4
    return pl.pallas_call(
        _ada_kernel,
        grid=(depth, n // tn),
        in_specs=[
            pl.BlockSpec((HALO, d), lambda i, j: (0, 0)),
            pl.BlockSpec((None, d, tn), lambda i, j: (i, 0, j)),
            pl.BlockSpec((None, 1, tn), lambda i, j: (i, 0, j)),
        ],
        out_specs=pl.BlockSpec((None, HALO, tn), lambda i, j: (i, 0, j)),
        out_shape=jax.ShapeDtypeStruct((depth, HALO, n), F32),
        compiler_params=_params(("parallel", "parallel")),
        name="ada_mods",
    )(cc, ada_w, ada_b.reshape(depth, 1, n))


def _qkv_kernel(ctx_ref, x_ref, sh_ref, sc_ref, g_ref, w_ref, qg_ref, kg_ref, cos_ref, sin_ref,
                gmat_ref, q_ref, kt_ref, v_ref, *, n_q, n_kv):
    t = pl.program_id(1)
    xin = jnp.where(t == 0, ctx_ref[...], x_ref[...])
    h = _rms(xin) * g_ref[...] * (1.0 + sc_ref[...]) + sh_ref[...]
    qkv = jnp.dot(h.astype(BF16), w_ref[...], preferred_element_type=F32)
    gmat = gmat_ref[...]
    cos = cos_ref[...]
    sin = sin_ref[...]
    lane = lax.broadcasted_iota(jnp.int32, cos.shape, 1)
    first_half = (lane % 32) < 16
    low = lane < HEAD_DIM
    nqb = n_q // LANES
    nkb = n_kv // LANES
    for j in range(nqb + nkb):
        xb = qkv[:, j * LANES:(j + 1) * LANES]
        ms = _group_sum_block(xb * xb, gmat) * (1.0 / HEAD_DIM)
        gain = qg_ref[...] if j < nqb else kg_ref[...]
        xn = xb * lax.rsqrt(ms + NORM_EPS) * gain
        partner = jnp.where(first_half, pltpu.roll(xn, LANES - 16, 1), pltpu.roll(xn, 16, 1))
        xr = xn * cos + partner * sin
        if j < nqb:
            q_ref[:, j * LANES:(j + 1) * LANES] = (xr * (HEAD_DIM ** -0.5)).astype(BF16)
        else:
            kt = xr.T
            jj = j - nqb
            ke = kt[:HEAD_DIM]
            ko = kt[HEAD_DIM:]
            kt_ref[2 * jj] = jnp.concatenate([ke, ke], axis=0).astype(BF16)
            kt_ref[2 * jj + 1] = jnp.concatenate([ko, ko], axis=0).astype(BF16)
    for j in range(nkb):
        vb = qkv[:, n_q + n_kv + j * LANES:n_q + n_kv + (j + 1) * LANES]
        sw = pltpu.roll(vb, HEAD_DIM, 1)
        v_ref[2 * j] = jnp.where(low, vb, sw).astype(BF16)
        v_ref[2 * j + 1] = jnp.where(low, sw, vb).astype(BF16)


def _mod_spec(layer, slot, n_batch, ctx_tiles, d):
    base = (layer * 6 + slot) * HALO

    def idx(b, t):
        return (base + jnp.where(t < ctx_tiles, n_batch, b), 0, 0)

    return pl.BlockSpec((None, 1, d), idx)


def _qkv_call(x, ctx, modsr, g, w_qkv, qg, kg, cos, sin, gmat, n_q, n_kv):
    bsz, s, d = x.shape
    l = ctx.shape[1]
    tm = ROW_TILE
    assert l == tm and s % tm == 0
    nt = (l + s) // tm
    t_all = l + s
    nkvh = n_kv // HEAD_DIM
    return pl.pallas_call(
        functools.partial(_qkv_kernel, n_q=n_q, n_kv=n_kv),
        grid=(bsz, nt),
        in_specs=[
            pl.BlockSpec((None, tm, d), lambda b, t: (b, 0, 0)),
            pl.BlockSpec((None, tm, d), lambda b, t: (b, jnp.maximum(t - 1, 0), 0)),
            _mod_spec(0, 0, bsz, 1, d),
            _mod_spec(0, 1, bsz, 1, d),
            _const_spec((1, d)),
            _const_spec(w_qkv.shape),
            _const_spec((1, LANES)),
            _const_spec((1, LANES)),
            pl.BlockSpec((tm, LANES), lambda b, t: (t, 0)),
            pl.BlockSpec((tm, LANES), lambda b, t: (t, 0)),
            _const_spec(gmat.shape),
        ],
        out_specs=[
            pl.BlockSpec((None, tm, n_q), lambda b, t: (b, t, 0)),
            pl.BlockSpec((None, nkvh, LANES, tm), lambda b, t: (b, 0, 0, t)),
            pl.BlockSpec((None, nkvh, tm, LANES), lambda b, t: (b, 0, t, 0)),
        ],
        out_shape=[
            jax.ShapeDtypeStruct((bsz, t_all, n_q), BF16),
            jax.ShapeDtypeStruct((bsz, nkvh, LANES, t_all), BF16),
            jax.ShapeDtypeStruct((bsz, nkvh, t_all, LANES), BF16),
        ],
        compiler_params=_params(("parallel", "arbitrary")),
        name="qkv_rope",
    )(ctx, x, modsr, modsr, g, w_qkv, qg, kg, cos, sin, gmat)


def _attn_kernel(sink_ref, q_ref, kc_ref, k0_ref, k1_ref, k2_ref, vc_ref, v0_ref, v1_ref, v2_ref,
                 o_ref, *, n_blocks, ctx_blocks, n_kv_heads):
    t = pl.program_id(1)
    bq = ATT_BLOCK
    m_rows = ATT_GROUP * bq
    row = lax.broadcasted_iota(jnp.int32, (m_rows, bq), 0) % bq
    col = lax.broadcasted_iota(jnp.int32, (m_rows, bq), 1)
    far = 4 * bq
    latent = t >= ctx_blocks
    keep0 = (col - row) >= jnp.where(t >= ctx_blocks + 1, 0, far)
    keep1 = (col - row) >= jnp.where(latent, -far, far)
    keep2 = (row - col) >= jnp.where(jnp.logical_and(latent, t <= n_blocks - 2), 0, far)
    lane = lax.broadcasted_iota(jnp.int32, (bq, LANES), 1)
    low = lane < HEAD_DIM
    for h in range(n_kv_heads):
        parts = []
        for g in range(ATT_GROUP):
            blk = 2 * h + g // 2
            qp = q_ref[:, blk * LANES:(blk + 1) * LANES]
            keep = low if g % 2 == 0 else jnp.logical_not(low)
            parts.append(jnp.where(keep, qp, jnp.zeros_like(qp)))
        qs = jnp.concatenate(parts, axis=0)
        sc = jnp.dot(qs, kc_ref[h], preferred_element_type=F32)
        s0 = jnp.where(keep0, jnp.dot(qs, k0_ref[h], preferred_element_type=F32), NEG_INF)
        s1 = jnp.where(keep1, jnp.dot(qs, k1_ref[h], preferred_element_type=F32), NEG_INF)
        s2 = jnp.where(keep2, jnp.dot(qs, k2_ref[h], preferred_element_type=F32), NEG_INF)
        sink = jnp.concatenate(
            [jnp.full((bq, 1), sink_ref[h * ATT_GROUP + g], F32) for g in range(ATT_GROUP)], axis=0)
        m = jnp.maximum(jnp.max(sc, axis=-1, keepdims=True), sink)
        for s_ in (s0, s1, s2):
            m = jnp.maximum(m, jnp.max(s_, axis=-1, keepdims=True))
        pc = jnp.exp(sc - m)
        p0 = jnp.exp(s0 - m)
        p1 = jnp.exp(s1 - m)
        p2 = jnp.exp(s2 - m)
        den = (jnp.sum(pc, axis=-1, keepdims=True) + jnp.sum(p0, axis=-1, keepdims=True)
               + jnp.sum(p1, axis=-1, keepdims=True) + jnp.sum(p2, axis=-1, keepdims=True)
               + jnp.exp(sink - m))
        o = (jnp.dot(pc.astype(BF16), vc_ref[h], preferred_element_type=F32)
             + jnp.dot(p0.astype(BF16), v0_ref[h], preferred_element_type=F32)
             + jnp.dot(p1.astype(BF16), v1_ref[h], preferred_element_type=F32)
             + jnp.dot(p2.astype(BF16), v2_ref[h], preferred_element_type=F32))
        o = o / den
        for jj in range(2):
            oe = o[(2 * jj) * bq:(2 * jj + 1) * bq]
            oo = o[(2 * jj + 1) * bq:(2 * jj + 2) * bq]
            blk = 2 * h + jj
            o_ref[:, blk * LANES:(blk + 1) * LANES] = jnp.where(low, oe, oo).astype(BF16)


def _attn_call(sink, q, kt, v, ctx_len):
    bsz, t_all, n_q = q.shape
    nkvh = kt.shape[1]
    bq = ATT_BLOCK
    nb = t_all // bq
    cb = ctx_len // bq
    lo = lambda t: jnp.maximum(t - 1, 0)
    hi = lambda t: jnp.minimum(t + 1, nb - 1)
    kspec = lambda f: pl.BlockSpec((None, nkvh, LANES, bq), lambda b, t: (b, 0, 0, f(t)))
    vspec = lambda f: pl.BlockSpec((None, nkvh, bq, LANES), lambda b, t: (b, 0, f(t), 0))
    return pl.pallas_call(
        functools.partial(_attn_kernel, n_blocks=nb, ctx_blocks=cb, n_kv_heads=nkvh),
        grid=(bsz, nb),
        in_specs=[
            pl.BlockSpec(memory_space=pltpu.SMEM),
            pl.BlockSpec((None, bq, n_q), lambda b, t: (b, t, 0)),
            pl.BlockSpec((None, nkvh, LANES, ctx_len), lambda b, t: (b, 0, 0, 0)),
            kspec(lo), kspec(lambda t: t), kspec(hi),
            pl.BlockSpec((None, nkvh, ctx_len, LANES), lambda b, t: (b, 0, 0, 0)),
            vspec(lo), vspec(lambda t: t), vspec(hi),
        ],
        out_specs=pl.BlockSpec((None, bq, n_q), lambda b, t: (b, t, 0)),
        out_shape=jax.ShapeDtypeStruct((bsz, t_all, n_q), BF16),
        compiler_params=_params(("parallel", "arbitrary")),
        name="window_attention",
    )(sink, q, kt, kt, kt, kt, v, v, v, v)


def _mlp_tail(x1, sh2, sc2, gt2, g2, w1_ref, w2_ref):
    h2 = (_rms(x1) * g2 * (1.0 + sc2) + sh2).astype(BF16)
    acc = jnp.zeros(x1.shape, F32)
    for c in range(w1_ref.shape[1] // FF_CHUNK):
        hid = jnp.dot(h2, w1_ref[:, c * FF_CHUNK:(c + 1) * FF_CHUNK], preferred_element_type=F32)
        hid = jnp.square(jnp.maximum(hid, 0.0)).astype(BF16)
        acc = acc + jnp.dot(hid, w2_ref[c * FF_CHUNK:(c + 1) * FF_CHUNK, :],
                            preferred_element_type=F32)
    return x1 + gt2 * acc


def _attn_out_mlp_kernel(o_ref, ctx_ref, x_ref, gt1_ref, sh2_ref, sc2_ref, gt2_ref, g2_ref,
                         wo_ref, w1_ref, w2_ref, out_ref):
    t = pl.program_id(1)
    x0 = jnp.where(t == 0, ctx_ref[...], x_ref[...])
    y = jnp.dot(o_ref[...], wo_ref[...], preferred_element_type=F32)
    x1 = x0 + gt1_ref[...] * y
    out_ref[...] = _mlp_tail(x1, sh2_ref[...], sc2_ref[...], gt2_ref[...], g2_ref[...],
                             w1_ref, w2_ref)


def _attn_out_mlp_call(o, x, ctx, modsr, g2, w_o, w1, w2):
    bsz, s, d = x.shape
    l = ctx.shape[1]
    tm = ROW_TILE
    nt = (l + s) // tm
    return pl.pallas_call(
        _attn_out_mlp_kernel,
        grid=(bsz, nt),
        in_specs=[
            pl.BlockSpec((None, tm, d), lambda b, t: (b, t, 0)),
            pl.BlockSpec((None, tm, d), lambda b, t: (b, 0, 0)),
            pl.BlockSpec((None, tm, d), lambda b, t: (b, jnp.maximum(t - 1, 0), 0)),
            _mod_spec(0, 2, bsz, 1, d),
            _mod_spec(0, 3, bsz, 1, d),
            _mod_spec(0, 4, bsz, 1, d),
            _mod_spec(0, 5, bsz, 1, d),
            _const_spec((1, d)),
            _const_spec(w_o.shape),
            _const_spec(w1.shape),
            _const_spec(w2.shape),
        ],
        out_specs=pl.BlockSpec((None, tm, d), lambda b, t: (b, t, 0)),
        out_shape=jax.ShapeDtypeStruct((bsz, l + s, d), F32),
        compiler_params=_params(("parallel", "arbitrary")),
        name="attn_out_mlp",
    )(o, ctx, x, modsr, modsr, modsr, modsr, g2, w_o, w1, w2)


def _rwkv_feat_kernel(x_ref, xp_ref, xn_ref, sh_ref, sc_ref, g_ref, mu_ref, wrkv_ref, w0_ref,
                      w1_ref, w2_ref, a0_ref, a1_ref, a2_ref, g1_ref, g2_ref, kk_ref, ka_ref,
                      rk_ref, gmat_ref,
                      r_out, k_out, v_out, a_out, b_out, lw0_out, lw1_out, bonus_out, gate0_out,
                      gate1_out, *, n_tiles, ctx_tiles):
    t = pl.program_id(1)
    d = x_ref.shape[1]
    tm = x_ref.shape[0]
    g = g_ref[...]
    scale = 1.0 + sc_ref[...]
    shift = sh_ref[...]

    def modulate(xv):
        return _rms(xv) * g * scale + shift

    h = modulate(x_ref[...])
    first = jnp.logical_or(t == 0, t == ctx_tiles)
    last = jnp.logical_or(t == ctx_tiles - 1, t == n_tiles - 1)
    prev_row = jnp.where(first, 0.0, modulate(xp_ref[...])[HALO - 1:HALO])
    next_row = jnp.where(last, 0.0, modulate(xn_ref[...])[0:1])
    rows = lax.broadcasted_iota(jnp.int32, (tm, d), 0)
    h_prev = jnp.where(rows == 0, prev_row, pltpu.roll(h, 1, 0))
    h_next = jnp.where(rows == tm - 1, next_row, pltpu.roll(h, tm - 1, 0))
    xx = 0.5 * (h_prev + h_next) - h

    def mix(m):
        return (h + xx * mu_ref[m:m + 1, :]).astype(BF16)

    gmat = gmat_ref[...]
    r = jnp.dot(mix(0), wrkv_ref[0], preferred_element_type=F32)
    k = jnp.dot(mix(2), wrkv_ref[1], preferred_element_type=F32)
    v = jnp.dot(mix(3), wrkv_ref[2], preferred_element_type=F32)
    dw = jnp.tanh(jnp.dot(mix(1), w1_ref[...], preferred_element_type=F32))
    z = w0_ref[...] + jnp.dot(dw.astype(BF16), w2_ref[...], preferred_element_type=F32)
    lw = -jax.nn.sigmoid(z) * math.exp(-0.5)
    lw0_out[...] = lw[:, :d]
    lw1_out[...] = lw[:, d:]
    al = jnp.dot(mix(4), a1_ref[...], preferred_element_type=F32)
    alr = jax.nn.sigmoid(a0_ref[...] + jnp.dot(al.astype(BF16), a2_ref[...],
                                               preferred_element_type=F32))
    kk = k * kk_ref[...]
    nrm = jnp.sqrt(_group_sum(kk * kk, gmat))
    kk = kk / jnp.maximum(nrm, 1e-12)
    k2 = k * (1.0 + (alr - 1.0) * ka_ref[...])
    r_out[...] = r
    k_out[...] = k2
    v_out[...] = v
    a_out[...] = -kk
    b_out[...] = kk * alr
    bonus_out[...] = _group_sum(r * k2 * rk_ref[...], gmat) * v
    gg = jax.nn.sigmoid(jnp.dot(mix(5), g1_ref[...], preferred_element_type=F32))
    gate = jnp.dot(gg.astype(BF16), g2_ref[...], preferred_element_type=F32)
    gate0_out[...] = gate[:, :d]
    gate1_out[...] = gate[:, d:]


def _rwkv_feat_call(xu, modsr, n_batch, ctx_len, g, mu, wrkv, w0c, w1c, w2bd, a0, a1, a2, g1c, g2bd,
                    k_k, k_a, r_k, gmat):
    bsz, t_all, d = xu.shape
    tm = ROW_TILE
    nt = t_all // tm
    ct = ctx_len // tm
    hb = tm // HALO
    tile = pl.BlockSpec((None, tm, d), lambda b, t: (b, t, 0))
    consts = [g, mu, wrkv, w0c, w1c, w2bd, a0, a1, a2, g1c, g2bd, k_k, k_a, r_k, gmat]
    return pl.pallas_call(
        functools.partial(_rwkv_feat_kernel, n_tiles=nt, ctx_tiles=ct),
        grid=(bsz, nt),
        in_specs=[
            tile,
            pl.BlockSpec((None, HALO, d), lambda b, t: (b, jnp.maximum(t * hb - 1, 0), 0)),
            pl.BlockSpec((None, HALO, d),
                         lambda b, t: (b, jnp.minimum((t + 1) * hb, t_all // HALO - 1), 0)),
            _mod_spec(1, 0, n_batch, ct, d),
            _mod_spec(1, 1, n_batch, ct, d),
        ] + [_const_spec(c.shape) for c in consts],
        out_specs=[tile] * 10,
        out_shape=[jax.ShapeDtypeStruct((bsz, t_all, d), F32)] * 10,
        compiler_params=_params(("parallel", "arbitrary")),
        name="rwkv_features",
    )(xu, xu, xu, modsr, modsr, *consts)


def _split3(x):
    h1 = x.astype(BF16)
    r1 = x - h1.astype(F32)
    h2 = r1.astype(BF16)
    h3 = (r1 - h2.astype(F32)).astype(BF16)
    return h1, h2, h3


def _scan_kernel(r_ref, k_ref, v_ref, a_ref, b_ref, lw_ref, y_ref, s_ref, *, reverse):
    @pl.when(pl.program_id(1) == 0)
    def _():
        s_ref[...] = jnp.zeros(s_ref.shape, F32)

    c, d = r_ref.shape
    gw = SCAN_GROUP * HEAD_DIM
    m = SCAN_GROUP * c
    groups = range(d // gw)
    ii = lax.broadcasted_iota(jnp.int32, (c, c), 0)
    jj = lax.broadcasted_iota(jnp.int32, (c, c), 1)
    tri = jnp.where((ii <= jj) if reverse else (ii >= jj), 1.0, 0.0).astype(BF16)
    lw = lw_ref[...]
    cs = sum(jnp.dot(tri, p, preferred_element_type=F32) for p in _split3(lw))
    tot = cs[0:1] if reverse else cs[c - 1:c]
    e_pos = jnp.exp(cs)
    e_neg = jnp.exp(-cs)
    e_rem = jnp.exp(tot - cs)
    e_tot = jnp.exp(tot)
    rv, kv, vv, av, bv = r_ref[...], k_ref[...], v_ref[...], a_ref[...], b_ref[...]
    a_t = av * jnp.exp(cs - lw)
    r_t = rv * e_pos
    b_t = bv * e_neg
    k_t = kv * e_neg
    b_p = bv * e_rem
    k_p = kv * e_rem

    row = lax.broadcasted_iota(jnp.int32, (c, m), 0)
    lane = lax.broadcasted_iota(jnp.int32, (c, m), 1)
    lt = lane % c
    if reverse:
        incl, strict = row <= lt, row < lt
    else:
        incl, strict = row >= lt, row > lt
    eye = jnp.where(row == lt, 1.0, 0.0)
    lane_head = lane // HEAD_DIM

    def expand(x):
        return jnp.concatenate(
            [jnp.where(lane_head == h, x, 0.0) for h in range(SCAN_GROUP)], axis=0)

    def bf(x):
        return x.astype(BF16)

    def mm(x, y):
        return jnp.dot(x, y, preferred_element_type=F32)

    def mm_nt(x, y):
        return lax.dot_general(x, y, (((1,), (1,)), ((), ())), preferred_element_type=F32)

    gs = [slice(g * gw, (g + 1) * gw) for g in groups]
    xa = [bf(jnp.concatenate([a_t[:, s], r_t[:, s]], axis=0)) for s in gs]
    wb = [bf(jnp.concatenate([expand(b_t[:, s]), expand(k_t[:, s])], axis=0)) for s in gs]
    vm = [expand(vv[:, s]) for s in gs]
    vmb = [bf(x) for x in vm]
    big = [mm_nt(xa[g], wb[g]) for g in groups]
    xs0 = [mm_nt(xa[g], bf(s_ref[g])) for g in groups]
    pw = [jnp.where(strict, big[g][:c, :m], 0.0) for g in groups]
    a_ak = [bf(jnp.where(strict, big[g][:c, m:], 0.0)) for g in groups]
    a_rb = [bf(jnp.where(incl, big[g][c:, :m], 0.0)) for g in groups]
    a_rk = [bf(jnp.where(incl, big[g][c:, m:], 0.0)) for g in groups]
    rhs = [xs0[g][:c] + mm(a_ak[g], vmb[g]) for g in groups]
    tinv = [eye + pw[g] for g in groups]
    span = 2
    first = True
    while span < c:
        last = 2 * span >= c
        nxt = []
        for g in groups:
            lhs = pw[g] if first else (tinv[g] if last else jnp.concatenate([pw[g], tinv[g]], axis=0))
            nxt.append(mm(bf(lhs), bf(expand(pw[g]))))
        if first:
            pw = nxt
        elif last:
            tinv = [tinv[g] + nxt[g] for g in groups]
        else:
            tinv = [tinv[g] + nxt[g][c:] for g in groups]
            pw = [nxt[g][:c] for g in groups]
        if not first:
            span *= 2
        first = False
    u = [mm(bf(tinv[g]), bf(expand(rhs[g]))) for g in groups]
    um = [expand(u[g]) for g in groups]
    for g in groups:
        y_ref[:, gs[g]] = xs0[g][c:] + mm(a_rb[g], bf(um[g])) + mm(a_rk[g], vmb[g])
    bpm = [bf(expand(b_p[:, s])) for s in gs]
    kpm = [bf(expand(k_p[:, s])) for s in gs]
    for g in groups:
        s_ref[g] = (s_ref[g] * e_tot[:, gs[g]] + mm(bf(um[g].T), bpm[g]) + mm(bf(vm[g].T), kpm[g]))


def _scan_call(r, k, v, a, b, lw, ctx_len, reverse):
    bsz, t_all, d = r.shape
    c = SCAN_CHUNK
    gw = SCAN_GROUP * HEAD_DIM
    nc = t_all // c
    cc = ctx_len // c
    if reverse:
        order = lambda i: jnp.where(i < cc, cc - 1 - i, nc - 1 - (i - cc))
    else:
        order = lambda i: i
    spec = pl.BlockSpec((None, c, d), lambda bb, i: (bb, order(i), 0))
    return pl.pallas_call(
        functools.partial(_scan_kernel, reverse=reverse),
        grid=(bsz, nc),
        in_specs=[spec] * 6,
        out_specs=spec,
        out_shape=jax.ShapeDtypeStruct((bsz, t_all, d), F32),
        scratch_shapes=[pltpu.VMEM((d // gw, gw, gw), F32)],
        compiler_params=_params(("parallel", "arbitrary")),
        name="wkv_scan_bwd" if reverse else "wkv_scan_fwd",
    )(r, k, v, a, b, lw)


def _group_norm(y, gmat, g, b):
    mean = _group_sum(y, gmat) * (1.0 / HEAD_DIM)
    yc = y - mean
    var = _group_sum(yc * yc, gmat) * (1.0 / HEAD_DIM)
    return yc * lax.rsqrt(var + GN_EPS) * g + b


def _rwkv_out_mlp_kernel(yf_ref, yb_ref, bonus_ref, g0_ref, g1_ref, x_ref, gt1_ref, sh2_ref,
                         sc2_ref, gt2_ref, gng_ref, gnb_ref, g2_ref, gmat_ref, wo_ref, w1_ref,
                         w2_ref, out_ref):
    gmat = gmat_ref[...]
    gng, gnb = gng_ref[...], gnb_ref[...]
    bonus = bonus_ref[...]
    o = ((_group_norm(yf_ref[...], gmat, gng, gnb) + bonus) * g0_ref[...]
         + (_group_norm(yb_ref[...], gmat, gng, gnb) + bonus) * g1_ref[...])
    y = jnp.dot(o.astype(BF16), wo_ref[...], preferred_element_type=F32)
    x1 = x_ref[...] + gt1_ref[...] * y
    out_ref[...] = _mlp_tail(x1, sh2_ref[...], sc2_ref[...], gt2_ref[...], g2_ref[...],
                             w1_ref, w2_ref)


def _rwkv_out_mlp_call(yf, yb, bonus, g0, g1, xu, modsr, ctx_len, gng, gnb, g2, gmat, w_o, w1, w2):
    bsz, t_all, d = xu.shape
    tm = ROW_TILE
    ct = ctx_len // tm
    ns = (t_all - ctx_len) // tm
    tile = pl.BlockSpec((None, tm, d), lambda b, t: (b, t + ct, 0))
    lat_mod = lambda slot: pl.BlockSpec((None, 1, d), lambda b, t: ((6 + slot) * HALO + b, 0, 0))
    consts = [gng, gnb, g2, gmat, w_o, w1, w2]
    return pl.pallas_call(
        _rwkv_out_mlp_kernel,
        grid=(bsz, ns),
        in_specs=[tile] * 6 + [lat_mod(2), lat_mod(3), lat_mod(4), lat_mod(5)]
        + [_const_spec(c.shape) for c in consts],
        out_specs=pl.BlockSpec((None, tm, d), lambda b, t: (b, t, 0)),
        out_shape=jax.ShapeDtypeStruct((bsz, t_all - ctx_len, d), F32),
        compiler_params=_params(("parallel", "arbitrary")),
        name="rwkv_out_mlp",
    )(yf, yb, bonus, g0, g1, xu, modsr, modsr, modsr, modsr, *consts)


def _rope_tables(ctx_len, s):
    half = HEAD_DIM // 4
    lane = jnp.arange(HEAD_DIM)
    freqs = ROPE_BASE ** (-(lane % half).astype(F32) / half)
    pos = jnp.arange(s)
    coord = jnp.where(lane[None, :] < HEAD_DIM // 2, (pos // GRID_W)[:, None], (pos % GRID_W)[:, None])
    ang = coord.astype(F32) * freqs[None, :]
    sign = jnp.where((lane % (2 * half)) < half, -1.0, 1.0)
    cos = jnp.concatenate([jnp.ones((ctx_len, HEAD_DIM), F32), jnp.cos(ang)], axis=0)
    sin = jnp.concatenate([jnp.zeros((ctx_len, HEAD_DIM), F32), jnp.sin(ang) * sign], axis=0)
    return jnp.tile(cos, (1, 2)), jnp.tile(sin, (1, 2))


def _block_diag2(w):
    z = jnp.zeros_like(w[0])
    return jnp.concatenate([jnp.concatenate([w[0], z], axis=1),
                            jnp.concatenate([z, w[1]], axis=1)], axis=0)


def kernel(x, c, ctx, c_ctx, ada_w, ada_b, norm1_g, norm2_g, mlp_w1, mlp_w2, att_w_qkv, att_q_gain, att_k_gain, att_sink, att_w_o, rwkv_mu, rwkv_w_rkv, rwkv_w0, rwkv_w1, rwkv_w2, rwkv_a0, rwkv_a1, rwkv_a2, rwkv_g1, rwkv_g2, rwkv_k_k, rwkv_k_a, rwkv_r_k, rwkv_gn_g, rwkv_gn_b, rwkv_w_o):
    bsz, s, d = x.shape
    l = ctx.shape[1]
    depth = ada_w.shape[0]
    assert depth == 2 and bsz < HALO and d % LANES == 0
    n_q = att_w_o.shape[1]
    n_kv = (att_w_qkv.shape[2] - n_q) // 2
    gmat = _group_matrix()

    cc = jnp.concatenate([c, c_ctx[None, :], jnp.zeros((HALO - bsz - 1, d), F32)], axis=0)
    mods = _ada_mods(cc, ada_w, ada_b)
    modsr = mods.reshape(depth, HALO, 6, d).transpose(0, 2, 1, 3).reshape(depth * 6 * HALO, 1, d)

    cos, sin = _rope_tables(l, s)
    q, kt, v = _qkv_call(x, ctx, modsr, norm1_g[0][None], att_w_qkv[0].astype(BF16),
                         jnp.tile(att_q_gain[0], 2)[None], jnp.tile(att_k_gain[0], 2)[None],
                         cos, sin, gmat, n_q, n_kv)
    o = _attn_call(att_sink[0], q, kt, v, l)
    xu = _attn_out_mlp_call(o, x, ctx, modsr, norm2_g[0][None], att_w_o[0].astype(BF16),
                            mlp_w1[0].astype(BF16), mlp_w2[0].astype(BF16))

    row = lambda a: a.reshape(1, -1)
    w1c = jnp.concatenate([rwkv_w1[0, 0], rwkv_w1[0, 1]], axis=1).astype(BF16)
    g1c = jnp.concatenate([rwkv_g1[0, 0], rwkv_g1[0, 1]], axis=1).astype(BF16)
    feats = _rwkv_feat_call(
        xu, modsr, bsz, l, norm1_g[1][None], rwkv_mu[0], rwkv_w_rkv[0].astype(BF16),
        row(rwkv_w0[0]), w1c, _block_diag2(rwkv_w2[0]).astype(BF16), row(rwkv_a0[0]),
        rwkv_a1[0].astype(BF16), rwkv_a2[0].astype(BF16), g1c,
        _block_diag2(rwkv_g2[0]).astype(BF16), row(rwkv_k_k[0]), row(rwkv_k_a[0]),
        row(rwkv_r_k[0]), gmat)
    r, k, vv, a, b, lw0, lw1, bonus, gate0, gate1 = feats
    y_fwd = _scan_call(r, k, vv, a, b, lw0, l, False)
    y_bwd = _scan_call(r, k, vv, a, b, lw1, l, True)
    return _rwkv_out_mlp_call(y_fwd, y_bwd, bonus, gate0, gate1, xu, modsr, l, row(rwkv_gn_g[0]),
                              row(rwkv_gn_b[0]), norm2_g[1][None], gmat, rwkv_w_o[0].astype(BF16),
                              mlp_w1[1].astype(BF16), mlp_w2[1].astype(BF16))
```

```python
import functools
import math

import jax
import jax.numpy as jnp
from jax import lax
from jax.experimental import pallas as pl
from jax.experimental.pallas import tpu as pltpu

F32 = jnp.float32
BF16 = jnp.bfloat16

HEAD_DIM = 64
LANES = 128
HALO = 8
ATT_GROUP = 4
ATT_BLOCK = 128
ATT_ROWS = 32
ROPE_BASE = 10000.0
GRID_W = 64
NORM_EPS = 1e-6
GN_EPS = 64e-5
NEG_INF = -1e30
SCAN_CHUNK = 64
SCAN_GROUP = 2
ROW_TILE = 256
FF_CHUNK = 1024
VMEM_LIMIT = 56 * 1024 * 1024


def _params(sem):
    return pltpu.CompilerParams(dimension_semantics=sem, vmem_limit_bytes=VMEM_LIMIT)


def _const_spec(shape):
    n = len(shape)
    return pl.BlockSpec(shape, lambda *_: (0,) * n, pipeline_mode=pl.Buffered(1))


def _rms(x):
    return x * lax.rsqrt(jnp.mean(x * x, axis=-1, keepdims=True) + NORM_EPS)


def _bdot(a, b):
    return jnp.dot(a.astype(BF16), b.astype(BF16), preferred_element_type=F32)


def _bdot_nt(a, b):
    return lax.dot_general(a.astype(BF16), b.astype(BF16), (((1,), (1,)), ((), ())),
                           preferred_element_type=F32)


def _group_sum_block(xb, gmat):
    hi = xb.astype(BF16)
    lo = (xb - hi.astype(F32)).astype(BF16)
    return jnp.dot(jnp.concatenate([hi, lo], axis=1), gmat, preferred_element_type=F32)


def _group_sum(x, gmat):
    n = x.shape[1] // LANES
    return jnp.concatenate(
        [_group_sum_block(x[:, j * LANES:(j + 1) * LANES], gmat) for j in range(n)], axis=1)


def _group_matrix():
    i = jnp.arange(2 * LANES)[:, None]
    j = jnp.arange(LANES)[None, :]
    return (((i % LANES) // HEAD_DIM) == (j // HEAD_DIM)).astype(BF16)


def _ada_kernel(c_ref, w_ref, b_ref, o_ref):
    c = c_ref[...]
    s = c * jax.nn.sigmoid(c)
    o_ref[...] = jnp.dot(s, w_ref[...], precision=lax.Precision.HIGHEST,
                         preferred_element_type=F32) + b_ref[...]


def _ada_mods(cc, ada_w, ada_b):
    depth, d, n = ada_w.shape
    tn = n // 4
    return pl.pallas_call(
        _ada_kernel,
        grid=(depth, n // tn),
        in_specs=[
            pl.BlockSpec((HALO, d), lambda i, j: (0, 0)),
            pl.BlockSpec((None, d, tn), lambda i, j: (i, 0, j)),
            pl.BlockSpec((None, 1, tn), lambda i, j: (i, 0, j)),
        ],
        out_specs=pl.BlockSpec((None, HALO, tn), lambda i, j: (i, 0, j)),
        out_shape=jax.ShapeDtypeStruct((depth, HALO, n), F32),
        compiler_params=_params(("parallel", "parallel")),
        name="ada_mods",
    )(cc, ada_w, ada_b.reshape(depth, 1, n))


def _qkv_kernel(ctx_ref, x_ref, sh_ref, sc_ref, g_ref, w_ref, qg_ref, kg_ref, cos_ref, sin_ref,
                gmat_ref, q_ref, kt_ref, v_ref, *, n_q, n_kv):
    t = pl.program_id(1)
    xin = jnp.where(t == 0, ctx_ref[...], x_ref[...])
    h = _rms(xin) * g_ref[...] * (1.0 + sc_ref[...]) + sh_ref[...]
    qkv = jnp.dot(h.astype(BF16), w_ref[...], preferred_element_type=F32)
    gmat = gmat_ref[...]
    cos = cos_ref[...]
    sin = sin_ref[...]
    lane = lax.broadcasted_iota(jnp.int32, cos.shape, 1)
    first_half = (lane % 32) < 16
    low = lane < HEAD_DIM
    nqb = n_q // LANES
    nkb = n_kv // LANES
    for j in range(nqb + nkb):
        xb = qkv[:, j * LANES:(j + 1) * LANES]
        ms = _group_sum_block(xb * xb, gmat) * (1.0 / HEAD_DIM)
        gain = qg_ref[...] if j < nqb else kg_ref[...]
        xn = xb * lax.rsqrt(ms + NORM_EPS) * gain
        partner = jnp.where(first_half, pltpu.roll(xn, LANES - 16, 1), pltpu.roll(xn, 16, 1))
        xr = xn * cos + partner * sin
        if j < nqb:
            q_ref[:, j * LANES:(j + 1) * LANES] = (xr * (HEAD_DIM ** -0.5)).astype(BF16)
        else:
            kt = xr.T
            jj = j - nqb
            ke = kt[:HEAD_DIM]
            ko = kt[HEAD_DIM:]
            kt_ref[2 * jj] = jnp.concatenate([ke, ke], axis=0).astype(BF16)
            kt_ref[2 * jj + 1] = jnp.concatenate([ko, ko], axis=0).astype(BF16)
    for j in range(nkb):
        vb = qkv[:, n_q + n_kv + j * LANES:n_q + n_kv + (j + 1) * LANES]
        sw = pltpu.roll(vb, HEAD_DIM, 1)
        v_ref[2 * j] = jnp.where(low, vb, sw).astype(BF16)
        v_ref[2 * j + 1] = jnp.where(low, sw, vb).astype(BF16)


def _mod_spec(layer, slot, n_batch, ctx_tiles, d):
    base = (layer * 6 + slot) * HALO

    def idx(b, t):
        return (base + jnp.where(t < ctx_tiles, n_batch, b), 0, 0)

    return pl.BlockSpec((None, 1, d), idx)


def _qkv_call(x, ctx, modsr, g, w_qkv, qg, kg, cos, sin, gmat, n_q, n_kv):
    bsz, s, d = x.shape
    l = ctx.shape[1]
    tm = ROW_TILE
    assert l == tm and s % tm == 0
    nt = (l + s) // tm
    t_all = l + s
    nkvh = n_kv // HEAD_DIM
    return pl.pallas_call(
        functools.partial(_qkv_kernel, n_q=n_q, n_kv=n_kv),
        grid=(bsz, nt),
        in_specs=[
            pl.BlockSpec((None, tm, d), lambda b, t: (b, 0, 0)),
            pl.BlockSpec((None, tm, d), lambda b, t: (b, jnp.maximum(t - 1, 0), 0)),
            _mod_spec(0, 0, bsz, 1, d),
            _mod_spec(0, 1, bsz, 1, d),
            _const_spec((1, d)),
            _const_spec(w_qkv.shape),
            _const_spec((1, LANES)),
            _const_spec((1, LANES)),
            pl.BlockSpec((tm, LANES), lambda b, t: (t, 0)),
            pl.BlockSpec((tm, LANES), lambda b, t: (t, 0)),
            _const_spec(gmat.shape),
        ],
        out_specs=[
            pl.BlockSpec((None, tm, n_q), lambda b, t: (b, t, 0)),
            pl.BlockSpec((None, nkvh, LANES, tm), lambda b, t: (b, 0, 0, t)),
            pl.BlockSpec((None, nkvh, tm, LANES), lambda b, t: (b, 0, t, 0)),
        ],
        out_shape=[
            jax.ShapeDtypeStruct((bsz, t_all, n_q), BF16),
            jax.ShapeDtypeStruct((bsz, nkvh, LANES, t_all), BF16),
            jax.ShapeDtypeStruct((bsz, nkvh, t_all, LANES), BF16),
        ],
        compiler_params=_params(("parallel", "arbitrary")),
        name="qkv_rope",
    )(ctx, x, modsr, modsr, g, w_qkv, qg, kg, cos, sin, gmat)


def _attn_kernel(sink_ref, q_ref, kc_ref, k0_ref, k1_ref, k2_ref, vc_ref, v0_ref, v1_ref, v2_ref,
                 o_ref, s_scr, p_scr, d_scr, *, n_blocks, ctx_blocks, n_kv_heads):
    t = pl.program_id(1)
    bq = ATT_BLOCK
    rb = ATT_ROWS
    lc = kc_ref.shape[-1]
    m_rows = ATT_GROUP * bq
    row = lax.broadcasted_iota(jnp.int32, (rb, bq), 0)
    col = lax.broadcasted_iota(jnp.int32, (rb, bq), 1)
    far = 4 * bq
    latent = t >= ctx_blocks
    thr0 = jnp.where(t >= ctx_blocks + 1, 0, far)
    thr1 = jnp.where(latent, -far, far)
    thr2 = jnp.where(jnp.logical_and(latent, t <= n_blocks - 2), 0, far)
    lane = lax.broadcasted_iota(jnp.int32, (bq, LANES), 1)
    low = lane < HEAD_DIM
    kv_refs = ((kc_ref, vc_ref, 0, lc), (k0_ref, v0_ref, lc, bq), (k1_ref, v1_ref, lc + bq, bq),
               (k2_ref, v2_ref, lc + 2 * bq, bq))
    for h in range(n_kv_heads):
        parts = []
        for g in range(ATT_GROUP):
            blk = 2 * h + g // 2
            qp = q_ref[:, blk * LANES:(blk + 1) * LANES]
            keep = low if g % 2 == 0 else jnp.logical_not(low)
            parts.append(jnp.where(keep, qp, jnp.zeros_like(qp)))
        qs = jnp.concatenate(parts, axis=0)
        for k_ref, _, off, width in kv_refs:
            s_scr[:, off:off + width] = jnp.dot(qs, k_ref[h], preferred_element_type=F32)

        def softmax_rows(i):
            r0 = i * rb
            rows = slice(r0, r0 + rb)
            iq = row + r0 % bq
            sink = sink_ref[h * ATT_GROUP + r0 // bq]
            sc = s_scr[rows, 0:lc]
            s0 = jnp.where((col - iq) >= thr0, s_scr[rows, lc:lc + bq], NEG_INF)
            s1 = jnp.where((col - iq) >= thr1, s_scr[rows, lc + bq:lc + 2 * bq], NEG_INF)
            s2 = jnp.where((iq - col) >= thr2, s_scr[rows, lc + 2 * bq:lc + 3 * bq], NEG_INF)
            mx = jnp.maximum(jnp.maximum(s0, s1), s2)
            for j in range(lc // bq):
                mx = jnp.maximum(mx, sc[:, j * bq:(j + 1) * bq])
            m = jnp.maximum(jnp.max(mx, axis=-1, keepdims=True), sink)
            pc = jnp.exp(sc - m)
            p0 = jnp.exp(s0 - m)
            p1 = jnp.exp(s1 - m)
            p2 = jnp.exp(s2 - m)
            ps = p0 + p1 + p2
            for j in range(lc // bq):
                ps = ps + pc[:, j * bq:(j + 1) * bq]
            den = jnp.sum(ps, axis=-1, keepdims=True) + jnp.exp(sink - m)
            p_scr[rows, 0:lc] = pc.astype(BF16)
            p_scr[rows, lc:lc + bq] = p0.astype(BF16)
            p_scr[rows, lc + bq:lc + 2 * bq] = p1.astype(BF16)
            p_scr[rows, lc + 2 * bq:lc + 3 * bq] = p2.astype(BF16)
            d_scr[rows, :] = jnp.broadcast_to(den, (rb, LANES))

        for i in range(m_rows // rb):
            softmax_rows(i)
        o = sum(jnp.dot(p_scr[:, off:off + width], v_ref[h], preferred_element_type=F32)
                for _, v_ref, off, width in kv_refs)
        o = o / d_scr[...]
        for jj in range(2):
            oe = o[(2 * jj) * bq:(2 * jj + 1) * bq]
            oo = o[(2 * jj + 1) * bq:(2 * jj + 2) * bq]
            blk = 2 * h + jj
            o_ref[:, blk * LANES:(blk + 1) * LANES] = jnp.where(low, oe, oo).astype(BF16)


def _attn_call(sink, q, kt, v, ctx_len):
    bsz, t_all, n_q = q.shape
    nkvh = kt.shape[1]
    bq = ATT_BLOCK
    nb = t_all // bq
    cb = ctx_len // bq
    lo = lambda t: jnp.maximum(t - 1, 0)
    hi = lambda t: jnp.minimum(t + 1, nb - 1)
    kspec = lambda f: pl.BlockSpec((None, nkvh, LANES, bq), lambda b, t: (b, 0, 0, f(t)))
    vspec = lambda f: pl.BlockSpec((None, nkvh, bq, LANES), lambda b, t: (b, 0, f(t), 0))
    return pl.pallas_call(
        functools.partial(_attn_kernel, n_blocks=nb, ctx_blocks=cb, n_kv_heads=nkvh),
        grid=(bsz, nb),
        in_specs=[
            pl.BlockSpec(memory_space=pltpu.SMEM),
            pl.BlockSpec((None, bq, n_q), lambda b, t: (b, t, 0)),
            pl.BlockSpec((None, nkvh, LANES, ctx_len), lambda b, t: (b, 0, 0, 0)),
            kspec(lo), kspec(lambda t: t), kspec(hi),
            pl.BlockSpec((None, nkvh, ctx_len, LANES), lambda b, t: (b, 0, 0, 0)),
            vspec(lo), vspec(lambda t: t), vspec(hi),
        ],
        out_specs=pl.BlockSpec((None, bq, n_q), lambda b, t: (b, t, 0)),
        out_shape=jax.ShapeDtypeStruct((bsz, t_all, n_q), BF16),
        scratch_shapes=[pltpu.VMEM((ATT_GROUP * bq, ctx_len + 3 * bq), F32),
                        pltpu.VMEM((ATT_GROUP * bq, ctx_len + 3 * bq), BF16),
                        pltpu.VMEM((ATT_GROUP * bq, LANES), F32)],
        compiler_params=_params(("parallel", "arbitrary")),
        name="window_attention",
    )(sink, q, kt, kt, kt, kt, v, v, v, v)


def _mlp_tail(x1, sh2, sc2, gt2, g2, w1_ref, w2_ref):
    h2 = (_rms(x1) * g2 * (1.0 + sc2) + sh2).astype(BF16)
    acc = jnp.zeros(x1.shape, F32)
    for c in range(w1_ref.shape[1] // FF_CHUNK):
        hid = jnp.dot(h2, w1_ref[:, c * FF_CHUNK:(c + 1) * FF_CHUNK], preferred_element_type=F32)
        hid = jnp.square(jnp.maximum(hid, 0.0)).astype(BF16)
        acc = acc + jnp.dot(hid, w2_ref[c * FF_CHUNK:(c + 1) * FF_CHUNK, :],
                            preferred_element_type=F32)
    return x1 + gt2 * acc


def _attn_out_mlp_kernel(o_ref, ctx_ref, x_ref, gt1_ref, sh2_ref, sc2_ref, gt2_ref, g2_ref,
                         wo_ref, w1_ref, w2_ref, out_ref):
    t = pl.program_id(1)
    x0 = jnp.where(t == 0, ctx_ref[...], x_ref[...])
    y = jnp.dot(o_ref[...], wo_ref[...], preferred_element_type=F32)
    x1 = x0 + gt1_ref[...] * y
    out_ref[...] = _mlp_tail(x1, sh2_ref[...], sc2_ref[...], gt2_ref[...], g2_ref[...],
                             w1_ref, w2_ref)


def _attn_out_mlp_call(o, x, ctx, modsr, g2, w_o, w1, w2):
    bsz, s, d = x.shape
    l = ctx.shape[1]
    tm = ROW_TILE
    nt = (l + s) // tm
    return pl.pallas_call(
        _attn_out_mlp_kernel,
        grid=(bsz, nt),
        in_specs=[
            pl.BlockSpec((None, tm, d), lambda b, t: (b, t, 0)),
            pl.BlockSpec((None, tm, d), lambda b, t: (b, 0, 0)),
            pl.BlockSpec((None, tm, d), lambda b, t: (b, jnp.maximum(t - 1, 0), 0)),
            _mod_spec(0, 2, bsz, 1, d),
            _mod_spec(0, 3, bsz, 1, d),
            _mod_spec(0, 4, bsz, 1, d),
            _mod_spec(0, 5, bsz, 1, d),
            _const_spec((1, d)),
            _const_spec(w_o.shape),
            _const_spec(w1.shape),
            _const_spec(w2.shape),
        ],
        out_specs=pl.BlockSpec((None, tm, d), lambda b, t: (b, t, 0)),
        out_shape=jax.ShapeDtypeStruct((bsz, l + s, d), F32),
        compiler_params=_params(("parallel", "arbitrary")),
        name="attn_out_mlp",
    )(o, ctx, x, modsr, modsr, modsr, modsr, g2, w_o, w1, w2)


def _rwkv_feat_kernel(x_ref, xp_ref, xn_ref, sh_ref, sc_ref, g_ref, mu_ref, wrkv_ref, w0_ref,
                      w1_ref, w2_ref, a0_ref, a1_ref, a2_ref, g1_ref, g2_ref, kk_ref, ka_ref,
                      rk_ref, gmat_ref,
                      r_out, k_out, v_out, a_out, b_out, lw0_out, lw1_out, bonus_out, gate0_out,
                      gate1_out, *, n_tiles, ctx_tiles):
    t = pl.program_id(1)
    d = x_ref.shape[1]
    tm = x_ref.shape[0]
    g = g_ref[...]
    scale = 1.0 + sc_ref[...]
    shift = sh_ref[...]

    def modulate(xv):
        return _rms(xv) * g * scale + shift

    h = modulate(x_ref[...])
    first = jnp.logical_or(t == 0, t == ctx_tiles)
    last = jnp.logical_or(t == ctx_tiles - 1, t == n_tiles - 1)
    prev_row = jnp.where(first, 0.0, modulate(xp_ref[...])[HALO - 1:HALO])
    next_row = jnp.where(last, 0.0, modulate(xn_ref[...])[0:1])
    rows = lax.broadcasted_iota(jnp.int32, (tm, d), 0)
    h_prev = jnp.where(rows == 0, prev_row, pltpu.roll(h, 1, 0))
    h_next = jnp.where(rows == tm - 1, next_row, pltpu.roll(h, tm - 1, 0))
    xx = 0.5 * (h_prev + h_next) - h

    def mix(m):
        return (h + xx * mu_ref[m:m + 1, :]).astype(BF16)

    gmat = gmat_ref[...]
    r = jnp.dot(mix(0), wrkv_ref[0], preferred_element_type=F32)
    k = jnp.dot(mix(2), wrkv_ref[1], preferred_element_type=F32)
    v = jnp.dot(mix(3), wrkv_ref[2], preferred_element_type=F32)
    dw = jnp.tanh(jnp.dot(mix(1), w1_ref[...], preferred_element_type=F32))
    z = w0_ref[...] + jnp.dot(dw.astype(BF16), w2_ref[...], preferred_element_type=F32)
    lw = -jax.nn.sigmoid(z) * math.exp(-0.5)
    lw0_out[...] = lw[:, :d]
    lw1_out[...] = lw[:, d:]
    al = jnp.dot(mix(4), a1_ref[...], preferred_element_type=F32)
    alr = jax.nn.sigmoid(a0_ref[...] + jnp.dot(al.astype(BF16), a2_ref[...],
                                               preferred_element_type=F32))
    kk = k * kk_ref[...]
    nrm = jnp.sqrt(_group_sum(kk * kk, gmat))
    kk = kk / jnp.maximum(nrm, 1e-12)
    k2 = k * (1.0 + (alr - 1.0) * ka_ref[...])
    r_out[...] = r
    k_out[...] = k2
    v_out[...] = v
    a_out[...] = -kk
    b_out[...] = kk * alr
    bonus_out[...] = _group_sum(r * k2 * rk_ref[...], gmat) * v
    gg = jax.nn.sigmoid(jnp.dot(mix(5), g1_ref[...], preferred_element_type=F32))
    gate = jnp.dot(gg.astype(BF16), g2_ref[...], preferred_element_type=F32)
    gate0_out[...] = gate[:, :d]
    gate1_out[...] = gate[:, d:]


def _rwkv_feat_call(xu, modsr, n_batch, ctx_len, g, mu, wrkv, w0c, w1c, w2bd, a0, a1, a2, g1c, g2bd,
                    k_k, k_a, r_k, gmat):
    bsz, t_all, d = xu.shape
    tm = ROW_TILE
    nt = t_all // tm
    ct = ctx_len // tm
    hb = tm // HALO
    tile = pl.BlockSpec((None, tm, d), lambda b, t: (b, t, 0))
    consts = [g, mu, wrkv, w0c, w1c, w2bd, a0, a1, a2, g1c, g2bd, k_k, k_a, r_k, gmat]
    return pl.pallas_call(
        functools.partial(_rwkv_feat_kernel, n_tiles=nt, ctx_tiles=ct),
        grid=(bsz, nt),
        in_specs=[
            tile,
            pl.BlockSpec((None, HALO, d), lambda b, t: (b, jnp.maximum(t * hb - 1, 0), 0)),
            pl.BlockSpec((None, HALO, d),
                         lambda b, t: (b, jnp.minimum((t + 1) * hb, t_all // HALO - 1), 0)),
            _mod_spec(1, 0, n_batch, ct, d),
            _mod_spec(1, 1, n_batch, ct, d),
        ] + [_const_spec(c.shape) for c in consts],
        out_specs=[tile] * 10,
        out_shape=[jax.ShapeDtypeStruct((bsz, t_all, d), F32)] * 10,
        compiler_params=_params(("parallel", "arbitrary")),
        name="rwkv_features",
    )(xu, xu, xu, modsr, modsr, *consts)


def _split3(x):
    h1 = x.astype(BF16)
    r1 = x - h1.astype(F32)
    h2 = r1.astype(BF16)
    h3 = (r1 - h2.astype(F32)).astype(BF16)
    return h1, h2, h3


def _scan_kernel(*refs):
    ins = (refs[0:6], refs[6:12])
    y_refs = refs[12:14]
    s_ref = refs[14]

    @pl.when(pl.program_id(1) == 0)
    def _():
        s_ref[...] = jnp.zeros(s_ref.shape, F32)

    c, d = y_refs[0].shape
    gw = SCAN_GROUP * HEAD_DIM
    m = SCAN_GROUP * c
    n_groups = d // gw
    ii = lax.broadcasted_iota(jnp.int32, (c, c), 0)
    jj = lax.broadcasted_iota(jnp.int32, (c, c), 1)
    row = lax.broadcasted_iota(jnp.int32, (c, m), 0)
    lane = lax.broadcasted_iota(jnp.int32, (c, m), 1)
    lt = lane % c
    eye = jnp.where(row == lt, 1.0, 0.0)
    lane_head = lane // HEAD_DIM

    def expand(x):
        return jnp.concatenate(
            [jnp.where(lane_head == h, x, 0.0) for h in range(SCAN_GROUP)], axis=0)

    def bf(x):
        return x.astype(BF16)

    def mm(x, y):
        return jnp.dot(x, y, preferred_element_type=F32)

    def mm_nt(x, y):
        return lax.dot_general(x, y, (((1,), (1,)), ((), ())), preferred_element_type=F32)

    jobs = []
    for direction in range(2):
        r_ref, k_ref, v_ref, a_ref, b_ref, lw_ref = ins[direction]
        reverse = direction == 1
        tri = jnp.where((ii <= jj) if reverse else (ii >= jj), 1.0, 0.0).astype(BF16)
        lw = lw_ref[...]
        cs = sum(jnp.dot(tri, p, preferred_element_type=F32) for p in _split3(lw))
        tot = cs[0:1] if reverse else cs[c - 1:c]
        e_pos = jnp.exp(cs)
        e_neg = jnp.exp(-cs)
        e_rem = jnp.exp(tot - cs)
        e_tot = jnp.exp(tot)
        rv, kv, vv, av, bv = r_ref[...], k_ref[...], v_ref[...], a_ref[...], b_ref[...]
        a_t = av * jnp.exp(cs - lw)
        r_t = rv * e_pos
        b_t = bv * e_neg
        k_t = kv * e_neg
        b_p = bv * e_rem
        k_p = kv * e_rem
        if reverse:
            incl, strict = row <= lt, row < lt
        else:
            incl, strict = row >= lt, row > lt
        for g in range(n_groups):
            sl = slice(g * gw, (g + 1) * gw)
            jobs.append(dict(
                sl=sl, si=(direction, g), y_ref=y_refs[direction], incl=incl, strict=strict,
                xa=bf(jnp.concatenate([a_t[:, sl], r_t[:, sl]], axis=0)),
                wb=bf(jnp.concatenate([expand(b_t[:, sl]), expand(k_t[:, sl])], axis=0)),
                v=vv[:, sl], b_p=b_p[:, sl], k_p=k_p[:, sl], e_tot=e_tot[:, sl]))
    for j in jobs:
        j["vmb"] = bf(expand(j["v"]))
        j["big"] = mm_nt(j["xa"], j["wb"])
    for j in jobs:
        j["xs0"] = mm_nt(j["xa"], bf(s_ref[j["si"]]))
    for j in jobs:
        big = j.pop("big")
        j["pw"] = jnp.where(j["strict"], big[:c, :m], 0.0)
        j["a_ak"] = bf(jnp.where(j["strict"], big[:c, m:], 0.0))
        j["a_rb"] = bf(jnp.where(j["incl"], big[c:, :m], 0.0))
        j["a_rk"] = bf(jnp.where(j["incl"], big[c:, m:], 0.0))
    for j in jobs:
        av = mm(jnp.concatenate([j["a_ak"], j["a_rk"]], axis=0), j["vmb"])
        j["rhs"] = j["xs0"][:c] + av[:c]
        j["yv"] = j["xs0"][c:] + av[c:]
        j["tinv"] = eye + j["pw"]
    for j in jobs:
        j["pw"] = mm(bf(j["pw"]), bf(expand(j["pw"])))
    span = 2
    while 2 * span < c:
        for j in jobs:
            nxt = mm(bf(jnp.concatenate([j["pw"], j["tinv"]], axis=0)), bf(expand(j["pw"])))
            j["pw"] = nxt[:c]
            j["tinv"] = j["tinv"] + nxt[c:]
        span *= 2
    for j in jobs:
        j["tinv"] = j["tinv"] + mm(bf(j["tinv"]), bf(expand(j["pw"])))
    for j in jobs:
        j["u"] = mm(bf(j["tinv"]), bf(expand(j["rhs"])))
    for j in jobs:
        j["y_ref"][:, j["sl"]] = j["yv"] + mm(j["a_rb"], bf(expand(j["u"])))
    same_head = (lax.broadcasted_iota(jnp.int32, (gw, gw), 0) // HEAD_DIM
                 == lax.broadcasted_iota(jnp.int32, (gw, gw), 1) // HEAD_DIM)
    for j in jobs:
        uv_t = jnp.concatenate([j["u"], j["v"]], axis=0).T
        upd = mm(bf(uv_t), bf(jnp.concatenate([j["b_p"], j["k_p"]], axis=0)))
        s_ref[j["si"]] = s_ref[j["si"]] * j["e_tot"] + jnp.where(same_head, upd, 0.0)


def _scan_call(r, k, v, a, b, lw_f, lw_b, ctx_len):
    bsz, t_all, d = r.shape
    c = SCAN_CHUNK
    assert c == HEAD_DIM
    gw = SCAN_GROUP * HEAD_DIM
    nc = t_all // c
    cc = ctx_len // c
    back = lambda i: jnp.where(i < cc, cc - 1 - i, nc - 1 - (i - cc))
    fspec = pl.BlockSpec((None, c, d), lambda bb, i: (bb, i, 0))
    bspec = pl.BlockSpec((None, c, d), lambda bb, i: (bb, back(i), 0))
    out = jax.ShapeDtypeStruct((bsz, t_all, d), F32)
    return pl.pallas_call(
        _scan_kernel,
        grid=(bsz, nc),
        in_specs=[fspec] * 6 + [bspec] * 6,
        out_specs=[fspec, bspec],
        out_shape=[out, out],
        scratch_shapes=[pltpu.VMEM((2, d // gw, gw, gw), F32)],
        compiler_params=_params(("parallel", "arbitrary")),
        name="wkv_scan",
    )(r, k, v, a, b, lw_f, r, k, v, a, b, lw_b)


def _group_norm(y, gmat, g, b):
    mean = _group_sum(y, gmat) * (1.0 / HEAD_DIM)
    yc = y - mean
    var = _group_sum(yc * yc, gmat) * (1.0 / HEAD_DIM)
    return yc * lax.rsqrt(var + GN_EPS) * g + b


def _rwkv_out_mlp_kernel(yf_ref, yb_ref, bonus_ref, g0_ref, g1_ref, x_ref, gt1_ref, sh2_ref,
                         sc2_ref, gt2_ref, gng_ref, gnb_ref, g2_ref, gmat_ref, wo_ref, w1_ref,
                         w2_ref, out_ref):
    gmat = gmat_ref[...]
    gng, gnb = gng_ref[...], gnb_ref[...]
    bonus = bonus_ref[...]
    o = ((_group_norm(yf_ref[...], gmat, gng, gnb) + bonus) * g0_ref[...]
         + (_group_norm(yb_ref[...], gmat, gng, gnb) + bonus) * g1_ref[...])
    y = jnp.dot(o.astype(BF16), wo_ref[...], preferred_element_type=F32)
    x1 = x_ref[...] + gt1_ref[...] * y
    out_ref[...] = _mlp_tail(x1, sh2_ref[...], sc2_ref[...], gt2_ref[...], g2_ref[...],
                             w1_ref, w2_ref)


def _rwkv_out_mlp_call(yf, yb, bonus, g0, g1, xu, modsr, ctx_len, gng, gnb, g2, gmat, w_o, w1, w2):
    bsz, t_all, d = xu.shape
    tm = ROW_TILE
    ct = ctx_len // tm
    ns = (t_all - ctx_len) // tm
    tile = pl.BlockSpec((None, tm, d), lambda b, t: (b, t + ct, 0))
    lat_mod = lambda slot: pl.BlockSpec((None, 1, d), lambda b, t: ((6 + slot) * HALO + b, 0, 0))
    consts = [gng, gnb, g2, gmat, w_o, w1, w2]
    return pl.pallas_call(
        _rwkv_out_mlp_kernel,
        grid=(bsz, ns),
        in_specs=[tile] * 6 + [lat_mod(2), lat_mod(3), lat_mod(4), lat_mod(5)]
        + [_const_spec(c.shape) for c in consts],
        out_specs=pl.BlockSpec((None, tm, d), lambda b, t: (b, t, 0)),
        out_shape=jax.ShapeDtypeStruct((bsz, t_all - ctx_len, d), F32),
        compiler_params=_params(("parallel", "arbitrary")),
        name="rwkv_out_mlp",
    )(yf, yb, bonus, g0, g1, xu, modsr, modsr, modsr, modsr, *consts)


def _rope_tables(ctx_len, s):
    half = HEAD_DIM // 4
    lane = jnp.arange(HEAD_DIM)
    freqs = ROPE_BASE ** (-(lane % half).astype(F32) / half)
    pos = jnp.arange(s)
    coord = jnp.where(lane[None, :] < HEAD_DIM // 2, (pos // GRID_W)[:, None], (pos % GRID_W)[:, None])
    ang = coord.astype(F32) * freqs[None, :]
    sign = jnp.where((lane % (2 * half)) < half, -1.0, 1.0)
    cos = jnp.concatenate([jnp.ones((ctx_len, HEAD_DIM), F32), jnp.cos(ang)], axis=0)
    sin = jnp.concatenate([jnp.zeros((ctx_len, HEAD_DIM), F32), jnp.sin(ang) * sign], axis=0)
    return jnp.tile(cos, (1, 2)), jnp.tile(sin, (1, 2))


def _block_diag2(w):
    z = jnp.zeros_like(w[0])
    return jnp.concatenate([jnp.concatenate([w[0], z], axis=1),
                            jnp.concatenate([z, w[1]], axis=1)], axis=0)


def kernel(x, c, ctx, c_ctx, ada_w, ada_b, norm1_g, norm2_g, mlp_w1, mlp_w2, att_w_qkv, att_q_gain, att_k_gain, att_sink, att_w_o, rwkv_mu, rwkv_w_rkv, rwkv_w0, rwkv_w1, rwkv_w2, rwkv_a0, rwkv_a1, rwkv_a2, rwkv_g1, rwkv_g2, rwkv_k_k, rwkv_k_a, rwkv_r_k, rwkv_gn_g, rwkv_gn_b, rwkv_w_o):
    bsz, s, d = x.shape
    l = ctx.shape[1]
    depth = ada_w.shape[0]
    assert depth == 2 and bsz < HALO and d % LANES == 0
    n_q = att_w_o.shape[1]
    n_kv = (att_w_qkv.shape[2] - n_q) // 2
    gmat = _group_matrix()

    cc = jnp.concatenate([c, c_ctx[None, :], jnp.zeros((HALO - bsz - 1, d), F32)], axis=0)
    mods = _ada_mods(cc, ada_w, ada_b)
    modsr = mods.reshape(depth, HALO, 6, d).transpose(0, 2, 1, 3).reshape(depth * 6 * HALO, 1, d)

    cos, sin = _rope_tables(l, s)
    q, kt, v = _qkv_call(x, ctx, modsr, norm1_g[0][None], att_w_qkv[0].astype(BF16),
                         jnp.tile(att_q_gain[0], 2)[None], jnp.tile(att_k_gain[0], 2)[None],
                         cos, sin, gmat, n_q, n_kv)
    o = _attn_call(att_sink[0], q, kt, v, l)
    xu = _attn_out_mlp_call(o, x, ctx, modsr, norm2_g[0][None], att_w_o[0].astype(BF16),
                            mlp_w1[0].astype(BF16), mlp_w2[0].astype(BF16))

    row = lambda a: a.reshape(1, -1)
    w1c = jnp.concatenate([rwkv_w1[0, 0], rwkv_w1[0, 1]], axis=1).astype(BF16)
    g1c = jnp.concatenate([rwkv_g1[0, 0], rwkv_g1[0, 1]], axis=1).astype(BF16)
    feats = _rwkv_feat_call(
        xu, modsr, bsz, l, norm1_g[1][None], rwkv_mu[0], rwkv_w_rkv[0].astype(BF16),
        row(rwkv_w0[0]), w1c, _block_diag2(rwkv_w2[0]).astype(BF16), row(rwkv_a0[0]),
        rwkv_a1[0].astype(BF16), rwkv_a2[0].astype(BF16), g1c,
        _block_diag2(rwkv_g2[0]).astype(BF16), row(rwkv_k_k[0]), row(rwkv_k_a[0]),
        row(rwkv_r_k[0]), gmat)
    r, k, vv, a, b, lw0, lw1, bonus, gate0, gate1 = feats
    y_fwd, y_bwd = _scan_call(r, k, vv, a, b, lw0, lw1, l)
    return _rwkv_out_mlp_call(y_fwd, y_bwd, bonus, gate0, gate1, xu, modsr, l, row(rwkv_gn_g[0]),
                              row(rwkv_gn_b[0]), norm2_g[1][None], gmat, rwkv_w_o[0].astype(BF16),
                              mlp_w1[1].astype(BF16), mlp_w2[1].astype(BF16))
```

```python
import functools
import math

import jax
import jax.numpy as jnp
from jax import lax
from jax.experimental import pallas as pl
from jax.experimental.pallas import tpu as pltpu

F32 = jnp.float32
BF16 = jnp.bfloat16

HEAD_DIM = 64
LANES = 128
HALO = 8
ATT_GROUP = 4
ATT_BLOCK = 128
ATT_ROWS = 32
ROPE_BASE = 10000.0
GRID_W = 64
NORM_EPS = 1e-6
GN_EPS = 64e-5
NEG_INF = -1e30
SCAN_CHUNK = 64
SCAN_GROUP = 2
ROW_TILE = 256
FEAT_COLS = 256
FF_CHUNK = 1024
VMEM_LIMIT = 56 * 1024 * 1024


def _params(sem):
    return pltpu.CompilerParams(dimension_semantics=sem, vmem_limit_bytes=VMEM_LIMIT)


def _const_spec(shape):
    n = len(shape)
    return pl.BlockSpec(shape, lambda *_: (0,) * n, pipeline_mode=pl.Buffered(1))


def _rms(x):
    return x * lax.rsqrt(jnp.mean(x * x, axis=-1, keepdims=True) + NORM_EPS)


def _bdot(a, b):
    return jnp.dot(a.astype(BF16), b.astype(BF16), preferred_element_type=F32)


def _bdot_nt(a, b):
    return lax.dot_general(a.astype(BF16), b.astype(BF16), (((1,), (1,)), ((), ())),
                           preferred_element_type=F32)


def _group_sum_block(xb, gmat):
    hi = xb.astype(BF16)
    lo = (xb - hi.astype(F32)).astype(BF16)
    return jnp.dot(jnp.concatenate([hi, lo], axis=1), gmat, preferred_element_type=F32)


def _group_sum(x, gmat):
    n = x.shape[1] // LANES
    return jnp.concatenate(
        [_group_sum_block(x[:, j * LANES:(j + 1) * LANES], gmat) for j in range(n)], axis=1)


def _group_matrix():
    i = jnp.arange(2 * LANES)[:, None]
    j = jnp.arange(LANES)[None, :]
    return (((i % LANES) // HEAD_DIM) == (j // HEAD_DIM)).astype(BF16)


def _ada_kernel(c_ref, w_ref, b_ref, o_ref):
    c = c_ref[...]
    s = c * jax.nn.sigmoid(c)
    o_ref[...] = jnp.dot(s, w_ref[...], precision=lax.Precision.HIGHEST,
                         preferred_element_type=F32) + b_ref[...]


def _ada_mods(cc, ada_w, ada_b):
    depth, d, n = ada_w.shape
    tn = n // 4
    return pl.pallas_call(
        _ada_kernel,
        grid=(depth, n // tn),
        in_specs=[
            pl.BlockSpec((HALO, d), lambda i, j: (0, 0)),
            pl.BlockSpec((None, d, tn), lambda i, j: (i, 0, j)),
            pl.BlockSpec((None, 1, tn), lambda i, j: (i, 0, j)),
        ],
        out_specs=pl.BlockSpec((None, HALO, tn), lambda i, j: (i, 0, j)),
        out_shape=jax.ShapeDtypeStruct((depth, HALO, n), F32),
        compiler_params=_params(("parallel", "parallel")),
        name="ada_mods",
    )(cc, ada_w, ada_b.reshape(depth, 1, n))


def _qkv_kernel(ctx_ref, x_ref, sh_ref, sc_ref, g_ref, w_ref, qg_ref, kg_ref, cos_ref, sin_ref,
                gmat_ref, q_ref, kt_ref, v_ref, *, n_q, n_kv):
    t = pl.program_id(1)
    xin = jnp.where(t == 0, ctx_ref[...], x_ref[...])
    h = _rms(xin) * g_ref[...] * (1.0 + sc_ref[...]) + sh_ref[...]
    hb = h.astype(BF16)
    gmat = gmat_ref[...]
    cos = cos_ref[...]
    sin = sin_ref[...]
    lane = lax.broadcasted_iota(jnp.int32, cos.shape, 1)
    first_half = (lane % 32) < 16
    low = lane < HEAD_DIM
    nqb = n_q // LANES
    nkb = n_kv // LANES
    cw = 2 * LANES
    chunk = lambda c: jnp.dot(hb, w_ref[:, c * cw:(c + 1) * cw], preferred_element_type=F32)
    n_chunks = w_ref.shape[1] // cw
    chunks = {0: chunk(0)}
    for j in range(nqb + nkb):
        if j % 2 == 0 and j // 2 + 1 < n_chunks:
            chunks[j // 2 + 1] = chunk(j // 2 + 1)
        xb = chunks[j // 2][:, (j % 2) * LANES:(j % 2 + 1) * LANES]
        ms = _group_sum_block(xb * xb, gmat) * (1.0 / HEAD_DIM)
        gain = qg_ref[...] if j < nqb else kg_ref[...]
        xn = xb * lax.rsqrt(ms + NORM_EPS) * gain
        partner = jnp.where(first_half, pltpu.roll(xn, LANES - 16, 1), pltpu.roll(xn, 16, 1))
        xr = xn * cos + partner * sin
        if j < nqb:
            q_ref[:, j * LANES:(j + 1) * LANES] = (xr * (HEAD_DIM ** -0.5)).astype(BF16)
        else:
            kt = xr.T
            jj = j - nqb
            ke = kt[:HEAD_DIM]
            ko = kt[HEAD_DIM:]
            kt_ref[2 * jj] = jnp.concatenate([ke, ke], axis=0).astype(BF16)
            kt_ref[2 * jj + 1] = jnp.concatenate([ko, ko], axis=0).astype(BF16)
    for j in range(nkb):
        jv = nqb + nkb + j
        vb = chunks[jv // 2][:, (jv % 2) * LANES:(jv % 2 + 1) * LANES]
        sw = pltpu.roll(vb, HEAD_DIM, 1)
        v_ref[2 * j] = jnp.where(low, vb, sw).astype(BF16)
        v_ref[2 * j + 1] = jnp.where(low, sw, vb).astype(BF16)


def _mod_spec(layer, slot, n_batch, ctx_tiles, d):
    base = (layer * 6 + slot) * HALO

    def idx(b, t):
        return (base + jnp.where(t < ctx_tiles, n_batch, b), 0, 0)

    return pl.BlockSpec((None, 1, d), idx)


def _qkv_call(x, ctx, modsr, g, w_qkv, qg, kg, cos, sin, gmat, n_q, n_kv):
    bsz, s, d = x.shape
    l = ctx.shape[1]
    tm = ROW_TILE
    assert l == tm and s % tm == 0
    nt = (l + s) // tm
    t_all = l + s
    nkvh = n_kv // HEAD_DIM
    return pl.pallas_call(
        functools.partial(_qkv_kernel, n_q=n_q, n_kv=n_kv),
        grid=(bsz, nt),
        in_specs=[
            pl.BlockSpec((None, tm, d), lambda b, t: (b, 0, 0)),
            pl.BlockSpec((None, tm, d), lambda b, t: (b, jnp.maximum(t - 1, 0), 0)),
            _mod_spec(0, 0, bsz, 1, d),
            _mod_spec(0, 1, bsz, 1, d),
            _const_spec((1, d)),
            _const_spec(w_qkv.shape),
            _const_spec((1, LANES)),
            _const_spec((1, LANES)),
            pl.BlockSpec((tm, LANES), lambda b, t: (t, 0)),
            pl.BlockSpec((tm, LANES), lambda b, t: (t, 0)),
            _const_spec(gmat.shape),
        ],
        out_specs=[
            pl.BlockSpec((None, tm, n_q), lambda b, t: (b, t, 0)),
            pl.BlockSpec((None, nkvh, LANES, tm), lambda b, t: (b, 0, 0, t)),
            pl.BlockSpec((None, nkvh, tm, LANES), lambda b, t: (b, 0, t, 0)),
        ],
        out_shape=[
            jax.ShapeDtypeStruct((bsz, t_all, n_q), BF16),
            jax.ShapeDtypeStruct((bsz, nkvh, LANES, t_all), BF16),
            jax.ShapeDtypeStruct((bsz, nkvh, t_all, LANES), BF16),
        ],
        compiler_params=_params(("parallel", "arbitrary")),
        name="qkv_rope",
    )(ctx, x, modsr, modsr, g, w_qkv, qg, kg, cos, sin, gmat)


def _attn_kernel(sink_ref, q_ref, kc_ref, k0_ref, k1_ref, k2_ref, vc_ref, v0_ref, v1_ref, v2_ref,
                 o_ref, s_scr, p_scr, d_scr, *, n_blocks, ctx_blocks, n_kv_heads):
    t = pl.program_id(1)
    bq = ATT_BLOCK
    rb = ATT_ROWS
    lc = kc_ref.shape[-1]
    m_rows = ATT_GROUP * bq
    row = lax.broadcasted_iota(jnp.int32, (rb, bq), 0)
    col = lax.broadcasted_iota(jnp.int32, (rb, bq), 1)
    far = 4 * bq
    latent = t >= ctx_blocks
    thr0 = jnp.where(t >= ctx_blocks + 1, 0, far)
    thr1 = jnp.where(latent, -far, far)
    thr2 = jnp.where(jnp.logical_and(latent, t <= n_blocks - 2), 0, far)
    lane = lax.broadcasted_iota(jnp.int32, (bq, LANES), 1)
    low = lane < HEAD_DIM
    kv_refs = ((kc_ref, vc_ref, 0, lc), (k0_ref, v0_ref, lc, bq), (k1_ref, v1_ref, lc + bq, bq),
               (k2_ref, v2_ref, lc + 2 * bq, bq))
    def scores(h):
        parts = []
        for g in range(ATT_GROUP):
            blk = 2 * h + g // 2
            qp = q_ref[:, blk * LANES:(blk + 1) * LANES]
            keep = low if g % 2 == 0 else jnp.logical_not(low)
            parts.append(jnp.where(keep, qp, jnp.zeros_like(qp)))
        qs = jnp.concatenate(parts, axis=0)
        for k_ref, _, off, width in kv_refs:
            s_scr[h % 2, :, off:off + width] = jnp.dot(qs, k_ref[h], preferred_element_type=F32)

    def softmax_rows(h, i):
        sb = h % 2
        r0 = i * rb
        rows = slice(r0, r0 + rb)
        iq = row + r0 % bq
        sink = sink_ref[h * ATT_GROUP + r0 // bq]
        sc = s_scr[sb, rows, 0:lc]
        s0 = jnp.where((col - iq) >= thr0, s_scr[sb, rows, lc:lc + bq], NEG_INF)
        s1 = jnp.where((col - iq) >= thr1, s_scr[sb, rows, lc + bq:lc + 2 * bq], NEG_INF)
        s2 = jnp.where((iq - col) >= thr2, s_scr[sb, rows, lc + 2 * bq:lc + 3 * bq], NEG_INF)
        mx = jnp.maximum(jnp.maximum(s0, s1), s2)
        for j in range(lc // bq):
            mx = jnp.maximum(mx, sc[:, j * bq:(j + 1) * bq])
        m = jnp.maximum(jnp.max(mx, axis=-1, keepdims=True), sink)
        pc = jnp.exp(sc - m)
        p0 = jnp.exp(s0 - m)
        p1 = jnp.exp(s1 - m)
        p2 = jnp.exp(s2 - m)
        ps = p0 + p1 + p2
        for j in range(lc // bq):
            ps = ps + pc[:, j * bq:(j + 1) * bq]
        den = jnp.sum(ps, axis=-1, keepdims=True) + jnp.exp(sink - m)
        p_scr[sb, rows, 0:lc] = pc.astype(BF16)
        p_scr[sb, rows, lc:lc + bq] = p0.astype(BF16)
        p_scr[sb, rows, lc + bq:lc + 2 * bq] = p1.astype(BF16)
        p_scr[sb, rows, lc + 2 * bq:lc + 3 * bq] = p2.astype(BF16)
        d_scr[sb, rows, :] = jnp.broadcast_to(den, (rb, LANES))

    def values(h):
        sb = h % 2
        o = sum(jnp.dot(p_scr[sb, :, off:off + width], v_ref[h], preferred_element_type=F32)
                for _, v_ref, off, width in kv_refs)
        o = o / d_scr[sb]
        for jj in range(2):
            oe = o[(2 * jj) * bq:(2 * jj + 1) * bq]
            oo = o[(2 * jj + 1) * bq:(2 * jj + 2) * bq]
            blk = 2 * h + jj
            o_ref[:, blk * LANES:(blk + 1) * LANES] = jnp.where(low, oe, oo).astype(BF16)

    scores(0)
    for h in range(n_kv_heads):
        if h + 1 < n_kv_heads:
            scores(h + 1)
        for i in range(m_rows // rb):
            softmax_rows(h, i)
        values(h)


def _attn_call(sink, q, kt, v, ctx_len):
    bsz, t_all, n_q = q.shape
    nkvh = kt.shape[1]
    bq = ATT_BLOCK
    nb = t_all // bq
    cb = ctx_len // bq
    lo = lambda t: jnp.maximum(t - 1, 0)
    hi = lambda t: jnp.minimum(t + 1, nb - 1)
    kspec = lambda f: pl.BlockSpec((None, nkvh, LANES, bq), lambda b, t: (b, 0, 0, f(t)))
    vspec = lambda f: pl.BlockSpec((None, nkvh, bq, LANES), lambda b, t: (b, 0, f(t), 0))
    return pl.pallas_call(
        functools.partial(_attn_kernel, n_blocks=nb, ctx_blocks=cb, n_kv_heads=nkvh),
        grid=(bsz, nb),
        in_specs=[
            pl.BlockSpec(memory_space=pltpu.SMEM),
            pl.BlockSpec((None, bq, n_q), lambda b, t: (b, t, 0)),
            pl.BlockSpec((None, nkvh, LANES, ctx_len), lambda b, t: (b, 0, 0, 0)),
            kspec(lo), kspec(lambda t: t), kspec(hi),
            pl.BlockSpec((None, nkvh, ctx_len, LANES), lambda b, t: (b, 0, 0, 0)),
            vspec(lo), vspec(lambda t: t), vspec(hi),
        ],
        out_specs=pl.BlockSpec((None, bq, n_q), lambda b, t: (b, t, 0)),
        out_shape=jax.ShapeDtypeStruct((bsz, t_all, n_q), BF16),
        scratch_shapes=[pltpu.VMEM((2, ATT_GROUP * bq, ctx_len + 3 * bq), F32),
                        pltpu.VMEM((2, ATT_GROUP * bq, ctx_len + 3 * bq), BF16),
                        pltpu.VMEM((2, ATT_GROUP * bq, LANES), F32)],
        compiler_params=_params(("parallel", "arbitrary")),
        name="window_attention",
    )(sink, q, kt, kt, kt, kt, v, v, v, v)


def _mlp_tail(x1, sh2, sc2, gt2, g2, w1_ref, w2_ref):
    h2 = (_rms(x1) * g2 * (1.0 + sc2) + sh2).astype(BF16)
    acc = jnp.zeros(x1.shape, F32)
    for c in range(w1_ref.shape[1] // FF_CHUNK):
        hid = jnp.dot(h2, w1_ref[:, c * FF_CHUNK:(c + 1) * FF_CHUNK], preferred_element_type=F32)
        hid = jnp.square(jnp.maximum(hid, 0.0)).astype(BF16)
        acc = acc + jnp.dot(hid, w2_ref[c * FF_CHUNK:(c + 1) * FF_CHUNK, :],
                            preferred_element_type=F32)
    return x1 + gt2 * acc


def _attn_out_mlp_kernel(o_ref, ctx_ref, x_ref, gt1_ref, sh2_ref, sc2_ref, gt2_ref, g2_ref,
                         wo_ref, w1_ref, w2_ref, out_ref):
    t = pl.program_id(1)
    x0 = jnp.where(t == 0, ctx_ref[...], x_ref[...])
    y = jnp.dot(o_ref[...], wo_ref[...], preferred_element_type=F32)
    x1 = x0 + gt1_ref[...] * y
    out_ref[...] = _mlp_tail(x1, sh2_ref[...], sc2_ref[...], gt2_ref[...], g2_ref[...],
                             w1_ref, w2_ref)


def _attn_out_mlp_call(o, x, ctx, modsr, g2, w_o, w1, w2):
    bsz, s, d = x.shape
    l = ctx.shape[1]
    tm = ROW_TILE
    nt = (l + s) // tm
    return pl.pallas_call(
        _attn_out_mlp_kernel,
        grid=(bsz, nt),
        in_specs=[
            pl.BlockSpec((None, tm, d), lambda b, t: (b, t, 0)),
            pl.BlockSpec((None, tm, d), lambda b, t: (b, 0, 0)),
            pl.BlockSpec((None, tm, d), lambda b, t: (b, jnp.maximum(t - 1, 0), 0)),
            _mod_spec(0, 2, bsz, 1, d),
            _mod_spec(0, 3, bsz, 1, d),
            _mod_spec(0, 4, bsz, 1, d),
            _mod_spec(0, 5, bsz, 1, d),
            _const_spec((1, d)),
            _const_spec(w_o.shape),
            _const_spec(w1.shape),
            _const_spec(w2.shape),
        ],
        out_specs=pl.BlockSpec((None, tm, d), lambda b, t: (b, t, 0)),
        out_shape=jax.ShapeDtypeStruct((bsz, l + s, d), F32),
        compiler_params=_params(("parallel", "arbitrary")),
        name="attn_out_mlp",
    )(o, ctx, x, modsr, modsr, modsr, modsr, g2, w_o, w1, w2)


def _rwkv_feat_kernel(x_ref, xp_ref, xn_ref, sh_ref, sc_ref, g_ref, mu_ref, wrkv_ref, w0_ref,
                      w1_ref, w2_ref, a0_ref, a1_ref, a2_ref, g1_ref, g2_ref, kk_ref, ka_ref,
                      rk_ref, gmat_ref,
                      r_out, k_out, v_out, a_out, b_out, lw0_out, lw1_out, bonus_out, gate0_out,
                      gate1_out, *, n_tiles, ctx_tiles):
    t = pl.program_id(1)
    d = x_ref.shape[1]
    tm = x_ref.shape[0]
    g = g_ref[...]
    scale = 1.0 + sc_ref[...]
    shift = sh_ref[...]

    def modulate(xv):
        return _rms(xv) * g * scale + shift

    h = modulate(x_ref[...])
    first = jnp.logical_or(t == 0, t == ctx_tiles)
    last = jnp.logical_or(t == ctx_tiles - 1, t == n_tiles - 1)
    prev_row = jnp.where(first, 0.0, modulate(xp_ref[...])[HALO - 1:HALO])
    next_row = jnp.where(last, 0.0, modulate(xn_ref[...])[0:1])
    rows = lax.broadcasted_iota(jnp.int32, (tm, d), 0)
    h_prev = jnp.where(rows == 0, prev_row, pltpu.roll(h, 1, 0))
    h_next = jnp.where(rows == tm - 1, next_row, pltpu.roll(h, tm - 1, 0))
    xx = 0.5 * (h_prev + h_next) - h

    def mix(m):
        return (h + xx * mu_ref[m:m + 1, :]).astype(BF16)

    gmat = gmat_ref[...]
    xr, xk, xv = mix(0), mix(2), mix(3)
    dw = jnp.tanh(jnp.dot(mix(1), w1_ref[...], preferred_element_type=F32)).astype(BF16)
    al = jnp.dot(mix(4), a1_ref[...], preferred_element_type=F32).astype(BF16)
    gg = jax.nn.sigmoid(jnp.dot(mix(5), g1_ref[...], preferred_element_type=F32)).astype(BF16)
    cw = FEAT_COLS
    n_chunks = d // cw

    def project(j):
        cs = slice(j * cw, (j + 1) * cw)
        cs1 = slice(d + j * cw, d + (j + 1) * cw)
        dot = lambda a, w: jnp.dot(a, w, preferred_element_type=F32)
        return dict(
            r=dot(xr, wrkv_ref[0, :, cs]), k=dot(xk, wrkv_ref[1, :, cs]), v=dot(xv, wrkv_ref[2, :, cs]),
            z0=dot(dw, w2_ref[:, cs]), z1=dot(dw, w2_ref[:, cs1]), a=dot(al, a2_ref[:, cs]),
            g0=dot(gg, g2_ref[:, cs]), g1=dot(gg, g2_ref[:, cs1]))

    def finish(j, p):
        cs = slice(j * cw, (j + 1) * cw)
        cs1 = slice(d + j * cw, d + (j + 1) * cw)
        lw0_out[:, cs] = -jax.nn.sigmoid(w0_ref[:, cs] + p["z0"]) * math.exp(-0.5)
        lw1_out[:, cs] = -jax.nn.sigmoid(w0_ref[:, cs1] + p["z1"]) * math.exp(-0.5)
        gate0_out[:, cs] = p["g0"]
        gate1_out[:, cs] = p["g1"]
        alr = jax.nn.sigmoid(a0_ref[:, cs] + p["a"])
        r, k, v = p["r"], p["k"], p["v"]
        kk = k * kk_ref[:, cs]
        nrm = jnp.sqrt(_group_sum(kk * kk, gmat))
        kk = kk / jnp.maximum(nrm, 1e-12)
        k2 = k * (1.0 + (alr - 1.0) * ka_ref[:, cs])
        r_out[:, cs] = r
        k_out[:, cs] = k2
        v_out[:, cs] = v
        a_out[:, cs] = -kk
        b_out[:, cs] = kk * alr
        bonus_out[:, cs] = _group_sum(r * k2 * rk_ref[:, cs], gmat) * v

    pending = project(0)
    for j in range(1, n_chunks):
        nxt = project(j)
        finish(j - 1, pending)
        pending = nxt
    finish(n_chunks - 1, pending)


def _rwkv_feat_call(xu, modsr, n_batch, ctx_len, g, mu, wrkv, w0c, w1c, w2bd, a0, a1, a2, g1c, g2bd,
                    k_k, k_a, r_k, gmat):
    bsz, t_all, d = xu.shape
    tm = ROW_TILE
    nt = t_all // tm
    ct = ctx_len // tm
    hb = tm // HALO
    tile = pl.BlockSpec((None, tm, d), lambda b, t: (b, t, 0))
    consts = [g, mu, wrkv, w0c, w1c, w2bd, a0, a1, a2, g1c, g2bd, k_k, k_a, r_k, gmat]
    return pl.pallas_call(
        functools.partial(_rwkv_feat_kernel, n_tiles=nt, ctx_tiles=ct),
        grid=(bsz, nt),
        in_specs=[
            tile,
            pl.BlockSpec((None, HALO, d), lambda b, t: (b, jnp.maximum(t * hb - 1, 0), 0)),
            pl.BlockSpec((None, HALO, d),
                         lambda b, t: (b, jnp.minimum((t + 1) * hb, t_all // HALO - 1), 0)),
            _mod_spec(1, 0, n_batch, ct, d),
            _mod_spec(1, 1, n_batch, ct, d),
        ] + [_const_spec(c.shape) for c in consts],
        out_specs=[tile] * 10,
        out_shape=[jax.ShapeDtypeStruct((bsz, t_all, d), F32)] * 10,
        compiler_params=_params(("parallel", "arbitrary")),
        name="rwkv_features",
    )(xu, xu, xu, modsr, modsr, *consts)


def _split3(x):
    h1 = x.astype(BF16)
    r1 = x - h1.astype(F32)
    h2 = r1.astype(BF16)
    h3 = (r1 - h2.astype(F32)).astype(BF16)
    return h1, h2, h3


def _scan_kernel(*refs):
    ins = (refs[0:6], refs[6:12])
    y_refs = refs[12:14]
    s_ref = refs[14]

    @pl.when(pl.program_id(1) == 0)
    def _():
        s_ref[...] = jnp.zeros(s_ref.shape, F32)

    c, d = y_refs[0].shape
    gw = SCAN_GROUP * HEAD_DIM
    m = SCAN_GROUP * c
    n_groups = d // gw
    ii = lax.broadcasted_iota(jnp.int32, (c, c), 0)
    jj = lax.broadcasted_iota(jnp.int32, (c, c), 1)
    row = lax.broadcasted_iota(jnp.int32, (c, m), 0)
    lane = lax.broadcasted_iota(jnp.int32, (c, m), 1)
    lt = lane % c
    eye = jnp.where(row == lt, 1.0, 0.0)
    lane_head = lane // HEAD_DIM

    def expand(x):
        return jnp.concatenate(
            [jnp.where(lane_head == h, x, 0.0) for h in range(SCAN_GROUP)], axis=0)

    def bf(x):
        return x.astype(BF16)

    def mm(x, y):
        return jnp.dot(x, y, preferred_element_type=F32)

    def mm_nt(x, y):
        return lax.dot_general(x, y, (((1,), (1,)), ((), ())), preferred_element_type=F32)

    jobs = []
    for direction in range(2):
        r_ref, k_ref, v_ref, a_ref, b_ref, lw_ref = ins[direction]
        reverse = direction == 1
        tri = jnp.where((ii <= jj) if reverse else (ii >= jj), 1.0, 0.0).astype(BF16)
        lw = lw_ref[...]
        cs = sum(jnp.dot(tri, p, preferred_element_type=F32) for p in _split3(lw))
        tot = cs[0:1] if reverse else cs[c - 1:c]
        e_pos = jnp.exp(cs)
        e_neg = jnp.exp(-cs)
        e_rem = jnp.exp(tot - cs)
        e_tot = jnp.exp(tot)
        rv, kv, vv, av, bv = r_ref[...], k_ref[...], v_ref[...], a_ref[...], b_ref[...]
        a_t = av * jnp.exp(cs - lw)
        r_t = rv * e_pos
        b_t = bv * e_neg
        k_t = kv * e_neg
        b_p = bv * e_rem
        k_p = kv * e_rem
        if reverse:
            incl, strict = row <= lt, row < lt
        else:
            incl, strict = row >= lt, row > lt
        for g in range(n_groups):
            sl = slice(g * gw, (g + 1) * gw)
            jobs.append(dict(
                sl=sl, si=(direction, g), y_ref=y_refs[direction], incl=incl, strict=strict,
                xa=bf(jnp.concatenate([a_t[:, sl], r_t[:, sl]], axis=0)),
                wb=bf(jnp.concatenate([expand(b_t[:, sl]), expand(k_t[:, sl])], axis=0)),
                v=vv[:, sl], b_p=b_p[:, sl], k_p=k_p[:, sl], e_tot=e_tot[:, sl]))
    for j in jobs:
        j["vmb"] = bf(expand(j["v"]))
        j["big"] = mm_nt(j["xa"], j["wb"])
    for j in jobs:
        j["xs0"] = mm_nt(j["xa"], bf(s_ref[j["si"]]))
    for j in jobs:
        big = j.pop("big")
        j["pw"] = jnp.where(j["strict"], big[:c, :m], 0.0)
        j["a_ak"] = bf(jnp.where(j["strict"], big[:c, m:], 0.0))
        j["a_rb"] = bf(jnp.where(j["incl"], big[c:, :m], 0.0))
        j["a_rk"] = bf(jnp.where(j["incl"], big[c:, m:], 0.0))
    for j in jobs:
        av = mm(jnp.concatenate([j["a_ak"], j["a_rk"]], axis=0), j["vmb"])
        j["rhs"] = j["xs0"][:c] + av[:c]
        j["yv"] = j["xs0"][c:] + av[c:]
        j["tinv"] = eye + j["pw"]
    for j in jobs:
        j["pw"] = mm(bf(j["pw"]), bf(expand(j["pw"])))
    span = 2
    while 2 * span < c:
        for j in jobs:
            nxt = mm(bf(jnp.concatenate([j["pw"], j["tinv"]], axis=0)), bf(expand(j["pw"])))
            j["pw"] = nxt[:c]
            j["tinv"] = j["tinv"] + nxt[c:]
        span *= 2
    for j in jobs:
        j["tinv"] = j["tinv"] + mm(bf(j["tinv"]), bf(expand(j["pw"])))
    for j in jobs:
        j["u"] = mm(bf(j["tinv"]), bf(expand(j["rhs"])))
    for j in jobs:
        j["y_ref"][:, j["sl"]] = j["yv"] + mm(j["a_rb"], bf(expand(j["u"])))
    same_head = (lax.broadcasted_iota(jnp.int32, (gw, gw), 0) // HEAD_DIM
                 == lax.broadcasted_iota(jnp.int32, (gw, gw), 1) // HEAD_DIM)
    for j in jobs:
        uv_t = jnp.concatenate([j["u"], j["v"]], axis=0).T
        upd = mm(bf(uv_t), bf(jnp.concatenate([j["b_p"], j["k_p"]], axis=0)))
        s_ref[j["si"]] = s_ref[j["si"]] * j["e_tot"] + jnp.where(same_head, upd, 0.0)


def _scan_call(r, k, v, a, b, lw_f, lw_b, ctx_len):
    bsz, t_all, d = r.shape
    c = SCAN_CHUNK
    assert c == HEAD_DIM
    gw = SCAN_GROUP * HEAD_DIM
    nc = t_all // c
    cc = ctx_len // c
    back = lambda i: jnp.where(i < cc, cc - 1 - i, nc - 1 - (i - cc))
    fspec = pl.BlockSpec((None, c, d), lambda bb, i: (bb, i, 0))
    bspec = pl.BlockSpec((None, c, d), lambda bb, i: (bb, back(i), 0))
    out = jax.ShapeDtypeStruct((bsz, t_all, d), F32)
    return pl.pallas_call(
        _scan_kernel,
        grid=(bsz, nc),
        in_specs=[fspec] * 6 + [bspec] * 6,
        out_specs=[fspec, bspec],
        out_shape=[out, out],
        scratch_shapes=[pltpu.VMEM((2, d // gw, gw, gw), F32)],
        compiler_params=_params(("parallel", "arbitrary")),
        name="wkv_scan",
    )(r, k, v, a, b, lw_f, r, k, v, a, b, lw_b)


def _group_norm(y, gmat, g, b):
    mean = _group_sum(y, gmat) * (1.0 / HEAD_DIM)
    yc = y - mean
    var = _group_sum(yc * yc, gmat) * (1.0 / HEAD_DIM)
    return yc * lax.rsqrt(var + GN_EPS) * g + b


def _rwkv_out_mlp_kernel(yf_ref, yb_ref, bonus_ref, g0_ref, g1_ref, x_ref, gt1_ref, sh2_ref,
                         sc2_ref, gt2_ref, gng_ref, gnb_ref, g2_ref, gmat_ref, wo_ref, w1_ref,
                         w2_ref, out_ref):
    gmat = gmat_ref[...]
    gng, gnb = gng_ref[...], gnb_ref[...]
    bonus = bonus_ref[...]
    o = ((_group_norm(yf_ref[...], gmat, gng, gnb) + bonus) * g0_ref[...]
         + (_group_norm(yb_ref[...], gmat, gng, gnb) + bonus) * g1_ref[...])
    y = jnp.dot(o.astype(BF16), wo_ref[...], preferred_element_type=F32)
    x1 = x_ref[...] + gt1_ref[...] * y
    out_ref[...] = _mlp_tail(x1, sh2_ref[...], sc2_ref[...], gt2_ref[...], g2_ref[...],
                             w1_ref, w2_ref)


def _rwkv_out_mlp_call(yf, yb, bonus, g0, g1, xu, modsr, ctx_len, gng, gnb, g2, gmat, w_o, w1, w2):
    bsz, t_all, d = xu.shape
    tm = ROW_TILE
    ct = ctx_len // tm
    ns = (t_all - ctx_len) // tm
    tile = pl.BlockSpec((None, tm, d), lambda b, t: (b, t + ct, 0))
    lat_mod = lambda slot: pl.BlockSpec((None, 1, d), lambda b, t: ((6 + slot) * HALO + b, 0, 0))
    consts = [gng, gnb, g2, gmat, w_o, w1, w2]
    return pl.pallas_call(
        _rwkv_out_mlp_kernel,
        grid=(bsz, ns),
        in_specs=[tile] * 6 + [lat_mod(2), lat_mod(3), lat_mod(4), lat_mod(5)]
        + [_const_spec(c.shape) for c in consts],
        out_specs=pl.BlockSpec((None, tm, d), lambda b, t: (b, t, 0)),
        out_shape=jax.ShapeDtypeStruct((bsz, t_all - ctx_len, d), F32),
        compiler_params=_params(("parallel", "arbitrary")),
        name="rwkv_out_mlp",
    )(yf, yb, bonus, g0, g1, xu, modsr, modsr, modsr, modsr, *consts)


def _rope_tables(ctx_len, s):
    half = HEAD_DIM // 4
    lane = jnp.arange(HEAD_DIM)
    freqs = ROPE_BASE ** (-(lane % half).astype(F32) / half)
    pos = jnp.arange(s)
    coord = jnp.where(lane[None, :] < HEAD_DIM // 2, (pos // GRID_W)[:, None], (pos % GRID_W)[:, None])
    ang = coord.astype(F32) * freqs[None, :]
    sign = jnp.where((lane % (2 * half)) < half, -1.0, 1.0)
    cos = jnp.concatenate([jnp.ones((ctx_len, HEAD_DIM), F32), jnp.cos(ang)], axis=0)
    sin = jnp.concatenate([jnp.zeros((ctx_len, HEAD_DIM), F32), jnp.sin(ang) * sign], axis=0)
    return jnp.tile(cos, (1, 2)), jnp.tile(sin, (1, 2))


def _block_diag2(w):
    z = jnp.zeros_like(w[0])
    return jnp.concatenate([jnp.concatenate([w[0], z], axis=1),
                            jnp.concatenate([z, w[1]], axis=1)], axis=0)


def kernel(x, c, ctx, c_ctx, ada_w, ada_b, norm1_g, norm2_g, mlp_w1, mlp_w2, att_w_qkv, att_q_gain, att_k_gain, att_sink, att_w_o, rwkv_mu, rwkv_w_rkv, rwkv_w0, rwkv_w1, rwkv_w2, rwkv_a0, rwkv_a1, rwkv_a2, rwkv_g1, rwkv_g2, rwkv_k_k, rwkv_k_a, rwkv_r_k, rwkv_gn_g, rwkv_gn_b, rwkv_w_o):
    bsz, s, d = x.shape
    l = ctx.shape[1]
    depth = ada_w.shape[0]
    assert depth == 2 and bsz < HALO and d % LANES == 0
    n_q = att_w_o.shape[1]
    n_kv = (att_w_qkv.shape[2] - n_q) // 2
    gmat = _group_matrix()

    cc = jnp.concatenate([c, c_ctx[None, :], jnp.zeros((HALO - bsz - 1, d), F32)], axis=0)
    mods = _ada_mods(cc, ada_w, ada_b)
    modsr = mods.reshape(depth, HALO, 6, d).transpose(0, 2, 1, 3).reshape(depth * 6 * HALO, 1, d)

    cos, sin = _rope_tables(l, s)
    q, kt, v = _qkv_call(x, ctx, modsr, norm1_g[0][None], att_w_qkv[0].astype(BF16),
                         jnp.tile(att_q_gain[0], 2)[None], jnp.tile(att_k_gain[0], 2)[None],
                         cos, sin, gmat, n_q, n_kv)
    o = _attn_call(att_sink[0], q, kt, v, l)
    xu = _attn_out_mlp_call(o, x, ctx, modsr, norm2_g[0][None], att_w_o[0].astype(BF16),
                            mlp_w1[0].astype(BF16), mlp_w2[0].astype(BF16))

    row = lambda a: a.reshape(1, -1)
    w1c = jnp.concatenate([rwkv_w1[0, 0], rwkv_w1[0, 1]], axis=1).astype(BF16)
    g1c = jnp.concatenate([rwkv_g1[0, 0], rwkv_g1[0, 1]], axis=1).astype(BF16)
    feats = _rwkv_feat_call(
        xu, modsr, bsz, l, norm1_g[1][None], rwkv_mu[0], rwkv_w_rkv[0].astype(BF16),
        row(rwkv_w0[0]), w1c, _block_diag2(rwkv_w2[0]).astype(BF16), row(rwkv_a0[0]),
        rwkv_a1[0].astype(BF16), rwkv_a2[0].astype(BF16), g1c,
        _block_diag2(rwkv_g2[0]).astype(BF16), row(rwkv_k_k[0]), row(rwkv_k_a[0]),
        row(rwkv_r_k[0]), gmat)
    r, k, vv, a, b, lw0, lw1, bonus, gate0, gate1 = feats
    y_fwd, y_bwd = _scan_call(r, k, vv, a, b, lw0, lw1, l)
    return _rwkv_out_mlp_call(y_fwd, y_bwd, bonus, gate0, gate1, xu, modsr, l, row(rwkv_gn_g[0]),
                              row(rwkv_gn_b[0]), norm2_g[1][None], gmat, rwkv_w_o[0].astype(BF16),
                              mlp_w1[1].astype(BF16), mlp_w2[1].astype(BF16))
```

```python
import functools
import math

import jax
import jax.numpy as jnp
from jax import lax
from jax.experimental import pallas as pl
from jax.experimental.pallas import tpu as pltpu

F32 = jnp.float32
BF16 = jnp.bfloat16

HEAD_DIM = 64
LANES = 128
HALO = 8
ATT_GROUP = 4
ATT_BLOCK = 128
ATT_ROWS = 32
ROPE_BASE = 10000.0
GRID_W = 64
NORM_EPS = 1e-6
GN_EPS = 64e-5
NEG_INF = -1e30
SCAN_CHUNK = 64
SCAN_GROUP = 2
SCAN_SUB = 4
ROW_TILE = 256
FEAT_COLS = 256
FF_CHUNK = 1024
VMEM_LIMIT = 56 * 1024 * 1024


def _params(sem):
    return pltpu.CompilerParams(dimension_semantics=sem, vmem_limit_bytes=VMEM_LIMIT)


def _const_spec(shape):
    n = len(shape)
    return pl.BlockSpec(shape, lambda *_: (0,) * n, pipeline_mode=pl.Buffered(1))


def _rms(x):
    return x * lax.rsqrt(jnp.mean(x * x, axis=-1, keepdims=True) + NORM_EPS)


def _group_sum_block(xb, gmat):
    hi = xb.astype(BF16)
    lo = (xb - hi.astype(F32)).astype(BF16)
    return jnp.dot(jnp.concatenate([hi, lo], axis=1), gmat, preferred_element_type=F32)


def _group_sum(x, gmat):
    n = x.shape[1] // LANES
    return jnp.concatenate(
        [_group_sum_block(x[:, j * LANES:(j + 1) * LANES], gmat) for j in range(n)], axis=1)


def _group_matrix():
    i = jnp.arange(2 * LANES)[:, None]
    j = jnp.arange(LANES)[None, :]
    return (((i % LANES) // HEAD_DIM) == (j // HEAD_DIM)).astype(BF16)


def _ada_kernel(c_ref, w_ref, b_ref, o_ref):
    c = c_ref[...]
    s = c * jax.nn.sigmoid(c)
    o_ref[...] = jnp.dot(s, w_ref[...], precision=lax.Precision.HIGHEST,
                         preferred_element_type=F32) + b_ref[...]


def _ada_mods(cc, ada_w, ada_b):
    depth, d, n = ada_w.shape
    tn = n // 4
    return pl.pallas_call(
        _ada_kernel,
        grid=(depth, n // tn),
        in_specs=[
            pl.BlockSpec((HALO, d), lambda i, j: (0, 0)),
            pl.BlockSpec((None, d, tn), lambda i, j: (i, 0, j)),
            pl.BlockSpec((None, 1, tn), lambda i, j: (i, 0, j)),
        ],
        out_specs=pl.BlockSpec((None, HALO, tn), lambda i, j: (i, 0, j)),
        out_shape=jax.ShapeDtypeStruct((depth, HALO, n), F32),
        compiler_params=_params(("parallel", "parallel")),
        name="ada_mods",
    )(cc, ada_w, ada_b.reshape(depth, 1, n))


def _qkv_kernel(ctx_ref, x_ref, sh_ref, sc_ref, g_ref, w_ref, qg_ref, kg_ref, cos_ref, sin_ref,
                gmat_ref, q_ref, kt_ref, v_ref, *, n_q, n_kv):
    t = pl.program_id(1)
    xin = jnp.where(t == 0, ctx_ref[...], x_ref[...])
    h = _rms(xin) * g_ref[...] * (1.0 + sc_ref[...]) + sh_ref[...]
    hb = h.astype(BF16)
    gmat = gmat_ref[...]
    cos = cos_ref[...]
    sin = sin_ref[...]
    lane = lax.broadcasted_iota(jnp.int32, cos.shape, 1)
    first_half = (lane % 32) < 16
    low = lane < HEAD_DIM
    nqb = n_q // LANES
    nkb = n_kv // LANES
    cw = 2 * LANES
    chunk = lambda c: jnp.dot(hb, w_ref[:, c * cw:(c + 1) * cw], preferred_element_type=F32)
    n_chunks = w_ref.shape[1] // cw
    chunks = {0: chunk(0)}
    for j in range(nqb + nkb):
        if j % 2 == 0 and j // 2 + 1 < n_chunks:
            chunks[j // 2 + 1] = chunk(j // 2 + 1)
        xb = chunks[j // 2][:, (j % 2) * LANES:(j % 2 + 1) * LANES]
        ms = _group_sum_block(xb * xb, gmat) * (1.0 / HEAD_DIM)
        gain = qg_ref[...] if j < nqb else kg_ref[...]
        xn = xb * lax.rsqrt(ms + NORM_EPS) * gain
        partner = jnp.where(first_half, pltpu.roll(xn, LANES - 16, 1), pltpu.roll(xn, 16, 1))
        xr = xn * cos + partner * sin
        if j < nqb:
            q_ref[:, j * LANES:(j + 1) * LANES] = (xr * (HEAD_DIM ** -0.5)).astype(BF16)
        else:
            kt = xr.T
            jj = j - nqb
            ke = kt[:HEAD_DIM]
            ko = kt[HEAD_DIM:]
            kt_ref[2 * jj] = jnp.concatenate([ke, ke], axis=0).astype(BF16)
            kt_ref[2 * jj + 1] = jnp.concatenate([ko, ko], axis=0).astype(BF16)
    for j in range(nkb):
        jv = nqb + nkb + j
        vb = chunks[jv // 2][:, (jv % 2) * LANES:(jv % 2 + 1) * LANES]
        sw = pltpu.roll(vb, HEAD_DIM, 1)
        v_ref[2 * j] = jnp.where(low, vb, sw).astype(BF16)
        v_ref[2 * j + 1] = jnp.where(low, sw, vb).astype(BF16)


def _mod_spec(layer, slot, n_batch, ctx_tiles, d):
    base = (layer * 6 + slot) * HALO

    def idx(b, t):
        return (base + jnp.where(t < ctx_tiles, n_batch, b), 0, 0)

    return pl.BlockSpec((None, 1, d), idx)


def _qkv_call(x, ctx, modsr, g, w_qkv, qg, kg, cos, sin, gmat, n_q, n_kv):
    bsz, s, d = x.shape
    l = ctx.shape[1]
    tm = ROW_TILE
    assert l == tm and s % tm == 0
    nt = (l + s) // tm
    t_all = l + s
    nkvh = n_kv // HEAD_DIM
    return pl.pallas_call(
        functools.partial(_qkv_kernel, n_q=n_q, n_kv=n_kv),
        grid=(bsz, nt),
        in_specs=[
            pl.BlockSpec((None, tm, d), lambda b, t: (b, 0, 0)),
            pl.BlockSpec((None, tm, d), lambda b, t: (b, jnp.maximum(t - 1, 0), 0)),
            _mod_spec(0, 0, bsz, 1, d),
            _mod_spec(0, 1, bsz, 1, d),
            _const_spec((1, d)),
            _const_spec(w_qkv.shape),
            _const_spec((1, LANES)),
            _const_spec((1, LANES)),
            pl.BlockSpec((tm, LANES), lambda b, t: (t, 0)),
            pl.BlockSpec((tm, LANES), lambda b, t: (t, 0)),
            _const_spec(gmat.shape),
        ],
        out_specs=[
            pl.BlockSpec((None, tm, n_q), lambda b, t: (b, t, 0)),
            pl.BlockSpec((None, nkvh, LANES, tm), lambda b, t: (b, 0, 0, t)),
            pl.BlockSpec((None, nkvh, tm, LANES), lambda b, t: (b, 0, t, 0)),
        ],
        out_shape=[
            jax.ShapeDtypeStruct((bsz, t_all, n_q), BF16),
            jax.ShapeDtypeStruct((bsz, nkvh, LANES, t_all), BF16),
            jax.ShapeDtypeStruct((bsz, nkvh, t_all, LANES), BF16),
        ],
        compiler_params=_params(("parallel", "arbitrary")),
        name="qkv_rope",
    )(ctx, x, modsr, modsr, g, w_qkv, qg, kg, cos, sin, gmat)


def _attn_kernel(sink_ref, q_ref, kc_ref, k0_ref, k1_ref, k2_ref, vc_ref, v0_ref, v1_ref, v2_ref,
                 o_ref, s_scr, p_scr, d_scr, *, n_blocks, ctx_blocks, n_kv_heads):
    t = pl.program_id(1)
    bq = ATT_BLOCK
    rb = ATT_ROWS
    lc = kc_ref.shape[-1]
    m_rows = ATT_GROUP * bq
    row = lax.broadcasted_iota(jnp.int32, (rb, bq), 0)
    col = lax.broadcasted_iota(jnp.int32, (rb, bq), 1)
    far = 4 * bq
    latent = t >= ctx_blocks
    thr0 = jnp.where(t >= ctx_blocks + 1, 0, far)
    thr1 = jnp.where(latent, -far, far)
    thr2 = jnp.where(jnp.logical_and(latent, t <= n_blocks - 2), 0, far)
    lane = lax.broadcasted_iota(jnp.int32, (bq, LANES), 1)
    low = lane < HEAD_DIM
    kv_refs = ((kc_ref, vc_ref, 0, lc), (k0_ref, v0_ref, lc, bq), (k1_ref, v1_ref, lc + bq, bq),
               (k2_ref, v2_ref, lc + 2 * bq, bq))
    def scores(h):
        parts = []
        for g in range(ATT_GROUP):
            blk = 2 * h + g // 2
            qp = q_ref[:, blk * LANES:(blk + 1) * LANES]
            keep = low if g % 2 == 0 else jnp.logical_not(low)
            parts.append(jnp.where(keep, qp, jnp.zeros_like(qp)))
        qs = jnp.concatenate(parts, axis=0)
        for k_ref, _, off, width in kv_refs:
            s_scr[h % 2, :, off:off + width] = jnp.dot(qs, k_ref[h], preferred_element_type=F32)

    def softmax_rows(h, i):
        sb = h % 2
        r0 = i * rb
        rows = slice(r0, r0 + rb)
        iq = row + r0 % bq
        sink = sink_ref[h * ATT_GROUP + r0 // bq]
        sc = s_scr[sb, rows, 0:lc]
        s0 = jnp.where((col - iq) >= thr0, s_scr[sb, rows, lc:lc + bq], NEG_INF)
        s1 = jnp.where((col - iq) >= thr1, s_scr[sb, rows, lc + bq:lc + 2 * bq], NEG_INF)
        s2 = jnp.where((iq - col) >= thr2, s_scr[sb, rows, lc + 2 * bq:lc + 3 * bq], NEG_INF)
        mx = jnp.maximum(jnp.maximum(s0, s1), s2)
        for j in range(lc // bq):
            mx = jnp.maximum(mx, sc[:, j * bq:(j + 1) * bq])
        m = jnp.maximum(jnp.max(mx, axis=-1, keepdims=True), sink)
        pc = jnp.exp(sc - m)
        p0 = jnp.exp(s0 - m)
        p1 = jnp.exp(s1 - m)
        p2 = jnp.exp(s2 - m)
        ps = p0 + p1 + p2
        for j in range(lc // bq):
            ps = ps + pc[:, j * bq:(j + 1) * bq]
        den = jnp.sum(ps, axis=-1, keepdims=True) + jnp.exp(sink - m)
        p_scr[sb, rows, 0:lc] = pc.astype(BF16)
        p_scr[sb, rows, lc:lc + bq] = p0.astype(BF16)
        p_scr[sb, rows, lc + bq:lc + 2 * bq] = p1.astype(BF16)
        p_scr[sb, rows, lc + 2 * bq:lc + 3 * bq] = p2.astype(BF16)
        d_scr[sb, rows, :] = jnp.broadcast_to(den, (rb, LANES))

    def values(h):
        sb = h % 2
        o = sum(jnp.dot(p_scr[sb, :, off:off + width], v_ref[h], preferred_element_type=F32)
                for _, v_ref, off, width in kv_refs)
        o = o / d_scr[sb]
        for jj in range(2):
            oe = o[(2 * jj) * bq:(2 * jj + 1) * bq]
            oo = o[(2 * jj + 1) * bq:(2 * jj + 2) * bq]
            blk = 2 * h + jj
            o_ref[:, blk * LANES:(blk + 1) * LANES] = jnp.where(low, oe, oo).astype(BF16)

    scores(0)
    for h in range(n_kv_heads):
        if h + 1 < n_kv_heads:
            scores(h + 1)
        for i in range(m_rows // rb):
            softmax_rows(h, i)
        values(h)


def _attn_call(sink, q, kt, v, ctx_len):
    bsz, t_all, n_q = q.shape
    nkvh = kt.shape[1]
    bq = ATT_BLOCK
    nb = t_all // bq
    cb = ctx_len // bq
    lo = lambda t: jnp.maximum(t - 1, 0)
    hi = lambda t: jnp.minimum(t + 1, nb - 1)
    kspec = lambda f: pl.BlockSpec((None, nkvh, LANES, bq), lambda b, t: (b, 0, 0, f(t)))
    vspec = lambda f: pl.BlockSpec((None, nkvh, bq, LANES), lambda b, t: (b, 0, f(t), 0))
    return pl.pallas_call(
        functools.partial(_attn_kernel, n_blocks=nb, ctx_blocks=cb, n_kv_heads=nkvh),
        grid=(bsz, nb),
        in_specs=[
            pl.BlockSpec(memory_space=pltpu.SMEM),
            pl.BlockSpec((None, bq, n_q), lambda b, t: (b, t, 0)),
            pl.BlockSpec((None, nkvh, LANES, ctx_len), lambda b, t: (b, 0, 0, 0)),
            kspec(lo), kspec(lambda t: t), kspec(hi),
            pl.BlockSpec((None, nkvh, ctx_len, LANES), lambda b, t: (b, 0, 0, 0)),
            vspec(lo), vspec(lambda t: t), vspec(hi),
        ],
        out_specs=pl.BlockSpec((None, bq, n_q), lambda b, t: (b, t, 0)),
        out_shape=jax.ShapeDtypeStruct((bsz, t_all, n_q), BF16),
        scratch_shapes=[pltpu.VMEM((2, ATT_GROUP * bq, ctx_len + 3 * bq), F32),
                        pltpu.VMEM((2, ATT_GROUP * bq, ctx_len + 3 * bq), BF16),
                        pltpu.VMEM((2, ATT_GROUP * bq, LANES), F32)],
        compiler_params=_params(("parallel", "arbitrary")),
        name="window_attention",
    )(sink, q, kt, kt, kt, kt, v, v, v, v)


def _mlp_tail(x1, sh2, sc2, gt2, g2, w1_ref, w2_ref):
    h2 = (_rms(x1) * g2 * (1.0 + sc2) + sh2).astype(BF16)
    acc = jnp.zeros(x1.shape, F32)
    for c in range(w1_ref.shape[1] // FF_CHUNK):
        hid = jnp.dot(h2, w1_ref[:, c * FF_CHUNK:(c + 1) * FF_CHUNK], preferred_element_type=F32)
        hid = jnp.square(jnp.maximum(hid, 0.0)).astype(BF16)
        acc = acc + jnp.dot(hid, w2_ref[c * FF_CHUNK:(c + 1) * FF_CHUNK, :],
                            preferred_element_type=F32)
    return x1 + gt2 * acc


def _attn_out_mlp_kernel(o_ref, ctx_ref, x_ref, gt1_ref, sh2_ref, sc2_ref, gt2_ref, g2_ref,
                         wo_ref, w1_ref, w2_ref, out_ref):
    t = pl.program_id(1)
    x0 = jnp.where(t == 0, ctx_ref[...], x_ref[...])
    y = jnp.dot(o_ref[...], wo_ref[...], preferred_element_type=F32)
    x1 = x0 + gt1_ref[...] * y
    out_ref[...] = _mlp_tail(x1, sh2_ref[...], sc2_ref[...], gt2_ref[...], g2_ref[...],
                             w1_ref, w2_ref)


def _attn_out_mlp_call(o, x, ctx, modsr, g2, w_o, w1, w2):
    bsz, s, d = x.shape
    l = ctx.shape[1]
    tm = ROW_TILE
    nt = (l + s) // tm
    return pl.pallas_call(
        _attn_out_mlp_kernel,
        grid=(bsz, nt),
        in_specs=[
            pl.BlockSpec((None, tm, d), lambda b, t: (b, t, 0)),
            pl.BlockSpec((None, tm, d), lambda b, t: (b, 0, 0)),
            pl.BlockSpec((None, tm, d), lambda b, t: (b, jnp.maximum(t - 1, 0), 0)),
            _mod_spec(0, 2, bsz, 1, d),
            _mod_spec(0, 3, bsz, 1, d),
            _mod_spec(0, 4, bsz, 1, d),
            _mod_spec(0, 5, bsz, 1, d),
            _const_spec((1, d)),
            _const_spec(w_o.shape),
            _const_spec(w1.shape),
            _const_spec(w2.shape),
        ],
        out_specs=pl.BlockSpec((None, tm, d), lambda b, t: (b, t, 0)),
        out_shape=jax.ShapeDtypeStruct((bsz, l + s, d), F32),
        compiler_params=_params(("parallel", "arbitrary")),
        name="attn_out_mlp",
    )(o, ctx, x, modsr, modsr, modsr, modsr, g2, w_o, w1, w2)


def _rwkv_feat_kernel(x_ref, xp_ref, xn_ref, sh_ref, sc_ref, g_ref, mu_ref, wrkv_ref, w0_ref,
                      w1_ref, w2_ref, a0_ref, a1_ref, a2_ref, g1_ref, g2_ref, kk_ref, ka_ref,
                      rk_ref, gmat_ref,
                      r_out, k_out, v_out, a_out, b_out, lw0_out, lw1_out, bonus_out, gate0_out,
                      gate1_out, *, n_tiles, ctx_tiles):
    t = pl.program_id(1)
    d = x_ref.shape[1]
    tm = x_ref.shape[0]
    g = g_ref[...]
    scale = 1.0 + sc_ref[...]
    shift = sh_ref[...]

    def modulate(xv):
        return _rms(xv) * g * scale + shift

    h = modulate(x_ref[...])
    first = jnp.logical_or(t == 0, t == ctx_tiles)
    last = jnp.logical_or(t == ctx_tiles - 1, t == n_tiles - 1)
    prev_row = jnp.where(first, 0.0, modulate(xp_ref[...])[HALO - 1:HALO])
    next_row = jnp.where(last, 0.0, modulate(xn_ref[...])[0:1])
    rows = lax.broadcasted_iota(jnp.int32, (tm, d), 0)
    h_prev = jnp.where(rows == 0, prev_row, pltpu.roll(h, 1, 0))
    h_next = jnp.where(rows == tm - 1, next_row, pltpu.roll(h, tm - 1, 0))
    xx = 0.5 * (h_prev + h_next) - h

    def mix(m):
        return (h + xx * mu_ref[m:m + 1, :]).astype(BF16)

    gmat = gmat_ref[...]
    xr, xk, xv = mix(0), mix(2), mix(3)
    dw = jnp.tanh(jnp.dot(mix(1), w1_ref[...], preferred_element_type=F32)).astype(BF16)
    al = jnp.dot(mix(4), a1_ref[...], preferred_element_type=F32).astype(BF16)
    gg = jax.nn.sigmoid(jnp.dot(mix(5), g1_ref[...], preferred_element_type=F32)).astype(BF16)
    cw = FEAT_COLS
    n_chunks = d // cw

    def project(j):
        cs = slice(j * cw, (j + 1) * cw)
        cs1 = slice(d + j * cw, d + (j + 1) * cw)
        dot = lambda a, w: jnp.dot(a, w, preferred_element_type=F32)
        return dict(
            r=dot(xr, wrkv_ref[0, :, cs]), k=dot(xk, wrkv_ref[1, :, cs]), v=dot(xv, wrkv_ref[2, :, cs]),
            z0=dot(dw, w2_ref[:, cs]), z1=dot(dw, w2_ref[:, cs1]), a=dot(al, a2_ref[:, cs]),
            g0=dot(gg, g2_ref[:, cs]), g1=dot(gg, g2_ref[:, cs1]))

    def finish(j, p):
        cs = slice(j * cw, (j + 1) * cw)
        cs1 = slice(d + j * cw, d + (j + 1) * cw)
        lw0_out[:, cs] = -jax.nn.sigmoid(w0_ref[:, cs] + p["z0"]) * math.exp(-0.5)
        lw1_out[:, cs] = -jax.nn.sigmoid(w0_ref[:, cs1] + p["z1"]) * math.exp(-0.5)
        gate0_out[:, cs] = p["g0"]
        gate1_out[:, cs] = p["g1"]
        alr = jax.nn.sigmoid(a0_ref[:, cs] + p["a"])
        r, k, v = p["r"], p["k"], p["v"]
        kk = k * kk_ref[:, cs]
        nrm = jnp.sqrt(_group_sum(kk * kk, gmat))
        kk = kk / jnp.maximum(nrm, 1e-12)
        k2 = k * (1.0 + (alr - 1.0) * ka_ref[:, cs])
        r_out[:, cs] = r
        k_out[:, cs] = k2
        v_out[:, cs] = v
        a_out[:, cs] = -kk
        b_out[:, cs] = kk * alr
        bonus_out[:, cs] = _group_sum(r * k2 * rk_ref[:, cs], gmat) * v

    pending = project(0)
    for j in range(1, n_chunks):
        nxt = project(j)
        finish(j - 1, pending)
        pending = nxt
    finish(n_chunks - 1, pending)


def _rwkv_feat_call(xu, modsr, n_batch, ctx_len, g, mu, wrkv, w0c, w1c, w2bd, a0, a1, a2, g1c, g2bd,
                    k_k, k_a, r_k, gmat):
    bsz, t_all, d = xu.shape
    tm = ROW_TILE
    nt = t_all // tm
    ct = ctx_len // tm
    hb = tm // HALO
    tile = pl.BlockSpec((None, tm, d), lambda b, t: (b, t, 0))
    consts = [g, mu, wrkv, w0c, w1c, w2bd, a0, a1, a2, g1c, g2bd, k_k, k_a, r_k, gmat]
    return pl.pallas_call(
        functools.partial(_rwkv_feat_kernel, n_tiles=nt, ctx_tiles=ct),
        grid=(bsz, nt),
        in_specs=[
            tile,
            pl.BlockSpec((None, HALO, d), lambda b, t: (b, jnp.maximum(t * hb - 1, 0), 0)),
            pl.BlockSpec((None, HALO, d),
                         lambda b, t: (b, jnp.minimum((t + 1) * hb, t_all // HALO - 1), 0)),
            _mod_spec(1, 0, n_batch, ct, d),
            _mod_spec(1, 1, n_batch, ct, d),
        ] + [_const_spec(c.shape) for c in consts],
        out_specs=[tile] * 10,
        out_shape=[jax.ShapeDtypeStruct((bsz, t_all, d), F32)] * 10,
        compiler_params=_params(("parallel", "arbitrary")),
        name="rwkv_features",
    )(xu, xu, xu, modsr, modsr, *consts)


def _scan_kernel(*refs):
    ins = (refs[0:6], refs[6:12])
    y_refs = refs[12:14]
    s_ref = refs[14]

    @pl.when(pl.program_id(1) == 0)
    def _():
        s_ref[...] = jnp.zeros(s_ref.shape, F32)

    c = SCAN_CHUNK
    d = y_refs[0].shape[1]
    n_sub = y_refs[0].shape[0] // c
    gw = SCAN_GROUP * HEAD_DIM
    m = SCAN_GROUP * c
    n_groups = d // gw
    rowd = lax.broadcasted_iota(jnp.int32, (c, d), 0)
    row = lax.broadcasted_iota(jnp.int32, (c, m), 0)
    lane = lax.broadcasted_iota(jnp.int32, (c, m), 1)
    lt = lane % c
    eye = jnp.where(row == lt, 1.0, 0.0)
    lane_head = lane // HEAD_DIM

    def expand(x):
        return jnp.concatenate(
            [jnp.where(lane_head == h, x, 0.0) for h in range(SCAN_GROUP)], axis=0)

    def bf(x):
        return x.astype(BF16)

    def mm(x, y):
        return jnp.dot(x, y, preferred_element_type=F32)

    def mm_nt(x, y):
        return lax.dot_general(x, y, (((1,), (1,)), ((), ())), preferred_element_type=F32)

    waves = [[] for _ in range(n_sub)]
    for direction in range(2):
        r_ref, k_ref, v_ref, a_ref, b_ref, lw_ref = ins[direction]
        reverse = direction == 1
        if reverse:
            incl, strict = row <= lt, row < lt
        else:
            incl, strict = row >= lt, row > lt
        for w in range(n_sub):
            sub = n_sub - 1 - w if reverse else w
            rows = slice(sub * c, (sub + 1) * c)
            lw = lw_ref[rows, :]
            cs = lw
            step = 1
            while step < c:
                if reverse:
                    moved = jnp.where(rowd < c - step, pltpu.roll(cs, c - step, 0), 0.0)
                else:
                    moved = jnp.where(rowd >= step, pltpu.roll(cs, step, 0), 0.0)
                cs = cs + moved
                step *= 2
            tot = cs[0:1] if reverse else cs[c - 1:c]
            e_pos = jnp.exp(cs)
            e_neg = jnp.exp(-cs)
            e_rem = jnp.exp(tot - cs)
            e_tot = jnp.exp(tot)
            rv, kv, vv, av, bv = (r_ref[rows, :], k_ref[rows, :], v_ref[rows, :], a_ref[rows, :],
                                  b_ref[rows, :])
            a_t = av * jnp.exp(cs - lw)
            r_t = rv * e_pos
            b_t = bv * e_neg
            k_t = kv * e_neg
            b_p = bv * e_rem
            k_p = kv * e_rem
            for g in range(n_groups):
                sl = slice(g * gw, (g + 1) * gw)
                waves[w].append(dict(
                    rows=rows, sl=sl, si=(direction, g), y_ref=y_refs[direction], incl=incl,
                    strict=strict,
                    xa=bf(jnp.concatenate([a_t[:, sl], r_t[:, sl]], axis=0)),
                    wb=bf(jnp.concatenate([expand(b_t[:, sl]), expand(k_t[:, sl])], axis=0)),
                    v=vv[:, sl], b_p=b_p[:, sl], k_p=k_p[:, sl], e_tot=e_tot[:, sl]))
    jobs = [j for wave in waves for j in wave]
    for j in jobs:
        j["vmb"] = bf(expand(j["v"]))
        j["big"] = mm_nt(j["xa"], j["wb"])
    for j in jobs:
        big = j.pop("big")
        j["pw"] = jnp.where(j["strict"], big[:c, :m], 0.0)
        j["a_ak"] = bf(jnp.where(j["strict"], big[:c, m:], 0.0))
        j["a_rb"] = bf(jnp.where(j["incl"], big[c:, :m], 0.0))
        j["a_rk"] = bf(jnp.where(j["incl"], big[c:, m:], 0.0))
    for j in jobs:
        j["av"] = mm(jnp.concatenate([j["a_ak"], j["a_rk"]], axis=0), j["vmb"])
        j["tinv"] = eye + j["pw"]
    for j in jobs:
        j["pw"] = mm(bf(j["pw"]), bf(expand(j["pw"])))
    span = 2
    while 2 * span < c:
        for j in jobs:
            nxt = mm(bf(jnp.concatenate([j["pw"], j["tinv"]], axis=0)), bf(expand(j["pw"])))
            j["pw"] = nxt[:c]
            j["tinv"] = j["tinv"] + nxt[c:]
        span *= 2
    for j in jobs:
        j["tinv"] = bf(j["tinv"] + mm(bf(j["tinv"]), bf(expand(j["pw"]))))
    same_head = (lax.broadcasted_iota(jnp.int32, (gw, gw), 0) // HEAD_DIM
                 == lax.broadcasted_iota(jnp.int32, (gw, gw), 1) // HEAD_DIM)
    for wave in waves:
        for j in wave:
            j["xs0"] = mm_nt(j["xa"], bf(s_ref[j["si"]]))
        for j in wave:
            j["u"] = mm(j["tinv"], bf(expand(j["xs0"][:c] + j["av"][:c])))
        for j in wave:
            j["y_ref"][j["rows"], j["sl"]] = (j["xs0"][c:] + j["av"][c:]
                                              + mm(j["a_rb"], bf(expand(j["u"]))))
        for j in wave:
            uv_t = jnp.concatenate([j["u"], j["v"]], axis=0).T
            upd = mm(bf(uv_t), bf(jnp.concatenate([j["b_p"], j["k_p"]], axis=0)))
            s_ref[j["si"]] = s_ref[j["si"]] * j["e_tot"] + jnp.where(same_head, upd, 0.0)


def _scan_call(r, k, v, a, b, lw_f, lw_b, ctx_len):
    bsz, t_all, d = r.shape
    c = SCAN_CHUNK
    assert c == HEAD_DIM
    rows = SCAN_SUB * c
    assert ctx_len % rows == 0 and t_all % rows == 0
    gw = SCAN_GROUP * HEAD_DIM
    nc = t_all // rows
    cc = ctx_len // rows
    back = lambda i: jnp.where(i < cc, cc - 1 - i, nc - 1 - (i - cc))
    fspec = pl.BlockSpec((None, rows, d), lambda bb, i: (bb, i, 0))
    bspec = pl.BlockSpec((None, rows, d), lambda bb, i: (bb, back(i), 0))
    out = jax.ShapeDtypeStruct((bsz, t_all, d), F32)
    return pl.pallas_call(
        _scan_kernel,
        grid=(bsz, nc),
        in_specs=[fspec] * 6 + [bspec] * 6,
        out_specs=[fspec, bspec],
        out_shape=[out, out],
        scratch_shapes=[pltpu.VMEM((2, d // gw, gw, gw), F32)],
        compiler_params=_params(("parallel", "arbitrary")),
        name="wkv_scan",
    )(r, k, v, a, b, lw_f, r, k, v, a, b, lw_b)


def _group_norm(y, gmat, g, b):
    mean = _group_sum(y, gmat) * (1.0 / HEAD_DIM)
    yc = y - mean
    var = _group_sum(yc * yc, gmat) * (1.0 / HEAD_DIM)
    return yc * lax.rsqrt(var + GN_EPS) * g + b


def _rwkv_out_mlp_kernel(yf_ref, yb_ref, bonus_ref, g0_ref, g1_ref, x_ref, gt1_ref, sh2_ref,
                         sc2_ref, gt2_ref, gng_ref, gnb_ref, g2_ref, gmat_ref, wo_ref, w1_ref,
                         w2_ref, out_ref):
    gmat = gmat_ref[...]
    gng, gnb = gng_ref[...], gnb_ref[...]
    bonus = bonus_ref[...]
    o = ((_group_norm(yf_ref[...], gmat, gng, gnb) + bonus) * g0_ref[...]
         + (_group_norm(yb_ref[...], gmat, gng, gnb) + bonus) * g1_ref[...])
    y = jnp.dot(o.astype(BF16), wo_ref[...], preferred_element_type=F32)
    x1 = x_ref[...] + gt1_ref[...] * y
    out_ref[...] = _mlp_tail(x1, sh2_ref[...], sc2_ref[...], gt2_ref[...], g2_ref[...],
                             w1_ref, w2_ref)


def _rwkv_out_mlp_call(yf, yb, bonus, g0, g1, xu, modsr, ctx_len, gng, gnb, g2, gmat, w_o, w1, w2):
    bsz, t_all, d = xu.shape
    tm = ROW_TILE
    ct = ctx_len // tm
    ns = (t_all - ctx_len) // tm
    tile = pl.BlockSpec((None, tm, d), lambda b, t: (b, t + ct, 0))
    lat_mod = lambda slot: pl.BlockSpec((None, 1, d), lambda b, t: ((6 + slot) * HALO + b, 0, 0))
    consts = [gng, gnb, g2, gmat, w_o, w1, w2]
    return pl.pallas_call(
        _rwkv_out_mlp_kernel,
        grid=(bsz, ns),
        in_specs=[tile] * 6 + [lat_mod(2), lat_mod(3), lat_mod(4), lat_mod(5)]
        + [_const_spec(c.shape) for c in consts],
        out_specs=pl.BlockSpec((None, tm, d), lambda b, t: (b, t, 0)),
        out_shape=jax.ShapeDtypeStruct((bsz, t_all - ctx_len, d), F32),
        compiler_params=_params(("parallel", "arbitrary")),
        name="rwkv_out_mlp",
    )(yf, yb, bonus, g0, g1, xu, modsr, modsr, modsr, modsr, *consts)


def _rope_tables(ctx_len, s):
    half = HEAD_DIM // 4
    lane = jnp.arange(HEAD_DIM)
    freqs = ROPE_BASE ** (-(lane % half).astype(F32) / half)
    pos = jnp.arange(s)
    coord = jnp.where(lane[None, :] < HEAD_DIM // 2, (pos // GRID_W)[:, None], (pos % GRID_W)[:, None])
    ang = coord.astype(F32) * freqs[None, :]
    sign = jnp.where((lane % (2 * half)) < half, -1.0, 1.0)
    cos = jnp.concatenate([jnp.ones((ctx_len, HEAD_DIM), F32), jnp.cos(ang)], axis=0)
    sin = jnp.concatenate([jnp.zeros((ctx_len, HEAD_DIM), F32), jnp.sin(ang) * sign], axis=0)
    return jnp.tile(cos, (1, 2)), jnp.tile(sin, (1, 2))


def _block_diag2(w):
    z = jnp.zeros_like(w[0])
    return jnp.concatenate([jnp.concatenate([w[0], z], axis=1),
                            jnp.concatenate([z, w[1]], axis=1)], axis=0)


def kernel(x, c, ctx, c_ctx, ada_w, ada_b, norm1_g, norm2_g, mlp_w1, mlp_w2, att_w_qkv, att_q_gain, att_k_gain, att_sink, att_w_o, rwkv_mu, rwkv_w_rkv, rwkv_w0, rwkv_w1, rwkv_w2, rwkv_a0, rwkv_a1, rwkv_a2, rwkv_g1, rwkv_g2, rwkv_k_k, rwkv_k_a, rwkv_r_k, rwkv_gn_g, rwkv_gn_b, rwkv_w_o):
    bsz, s, d = x.shape
    l = ctx.shape[1]
    depth = ada_w.shape[0]
    assert depth == 2 and bsz < HALO and d % LANES == 0
    n_q = att_w_o.shape[1]
    n_kv = (att_w_qkv.shape[2] - n_q) // 2
    gmat = _group_matrix()

    cc = jnp.concatenate([c, c_ctx[None, :], jnp.zeros((HALO - bsz - 1, d), F32)], axis=0)
    mods = _ada_mods(cc, ada_w, ada_b)
    modsr = mods.reshape(depth, HALO, 6, d).transpose(0, 2, 1, 3).reshape(depth * 6 * HALO, 1, d)

    cos, sin = _rope_tables(l, s)
    q, kt, v = _qkv_call(x, ctx, modsr, norm1_g[0][None], att_w_qkv[0].astype(BF16),
                         jnp.tile(att_q_gain[0], 2)[None], jnp.tile(att_k_gain[0], 2)[None],
                         cos, sin, gmat, n_q, n_kv)
    o = _attn_call(att_sink[0], q, kt, v, l)
    xu = _attn_out_mlp_call(o, x, ctx, modsr, norm2_g[0][None], att_w_o[0].astype(BF16),
                            mlp_w1[0].astype(BF16), mlp_w2[0].astype(BF16))

    row = lambda a: a.reshape(1, -1)
    w1c = jnp.concatenate([rwkv_w1[0, 0], rwkv_w1[0, 1]], axis=1).astype(BF16)
    g1c = jnp.concatenate([rwkv_g1[0, 0], rwkv_g1[0, 1]], axis=1).astype(BF16)
    feats = _rwkv_feat_call(
        xu, modsr, bsz, l, norm1_g[1][None], rwkv_mu[0], rwkv_w_rkv[0].astype(BF16),
        row(rwkv_w0[0]), w1c, _block_diag2(rwkv_w2[0]).astype(BF16), row(rwkv_a0[0]),
        rwkv_a1[0].astype(BF16), rwkv_a2[0].astype(BF16), g1c,
        _block_diag2(rwkv_g2[0]).astype(BF16), row(rwkv_k_k[0]), row(rwkv_k_a[0]),
        row(rwkv_r_k[0]), gmat)
    r, k, vv, a, b, lw0, lw1, bonus, gate0, gate1 = feats
    y_fwd, y_bwd = _scan_call(r, k, vv, a, b, lw0, lw1, l)
    return _rwkv_out_mlp_call(y_fwd, y_bwd, bonus, gate0, gate1, xu, modsr, l, row(rwkv_gn_g[0]),
                              row(rwkv_gn_b[0]), norm2_g[1][None], gmat, rwkv_w_o[0].astype(BF16),
                              mlp_w1[1].astype(BF16), mlp_w2[1].astype(BF16))
```

```python
import functools
import math

import jax
import jax.numpy as jnp
from jax import lax
from jax.experimental import pallas as pl
from jax.experimental.pallas import tpu as pltpu

F32 = jnp.float32
BF16 = jnp.bfloat16

HEAD_DIM = 64
LANES = 128
HALO = 8
ATT_GROUP = 4
ATT_BLOCK = 128
ATT_SUB = 3
ATT_ROWS = 32
ROPE_BASE = 10000.0
GRID_W = 64
NORM_EPS = 1e-6
GN_EPS = 64e-5
NEG_INF = -1e30
LOG2E = math.log2(math.e)
Q_SCALE = HEAD_DIM ** -0.5 * LOG2E
SCAN_CHUNK = 64
SCAN_GROUP = 2
SCAN_SUB = 4
ROW_TILE = 256
FEAT_COLS = 256
FF_CHUNK = 1024
VMEM_LIMIT = 56 * 1024 * 1024


def _params(sem):
    return pltpu.CompilerParams(dimension_semantics=sem, vmem_limit_bytes=VMEM_LIMIT)


def _const_spec(shape):
    n = len(shape)
    return pl.BlockSpec(shape, lambda *_: (0,) * n, pipeline_mode=pl.Buffered(1))


def _rms(x):
    return x * lax.rsqrt(jnp.mean(x * x, axis=-1, keepdims=True) + NORM_EPS)


def _group_sum_block(xb, gmat):
    hi = xb.astype(BF16)
    lo = (xb - hi.astype(F32)).astype(BF16)
    return jnp.dot(jnp.concatenate([hi, lo], axis=1), gmat, preferred_element_type=F32)


def _group_sum(x, gmat):
    n = x.shape[1] // LANES
    return jnp.concatenate(
        [_group_sum_block(x[:, j * LANES:(j + 1) * LANES], gmat) for j in range(n)], axis=1)


def _group_matrix():
    i = jnp.arange(2 * LANES)[:, None]
    j = jnp.arange(LANES)[None, :]
    return (((i % LANES) // HEAD_DIM) == (j // HEAD_DIM)).astype(BF16)


def _ada_kernel(c_ref, w_ref, b_ref, o_ref):
    c = c_ref[...]
    s = c * jax.nn.sigmoid(c)
    o_ref[...] = jnp.dot(s, w_ref[...], precision=lax.Precision.HIGHEST,
                         preferred_element_type=F32) + b_ref[...]


def _ada_mods(cc, ada_w, ada_b):
    depth, d, n = ada_w.shape
    tn = n // 4
    return pl.pallas_call(
        _ada_kernel,
        grid=(depth, n // tn),
        in_specs=[
            pl.BlockSpec((HALO, d), lambda i, j: (0, 0)),
            pl.BlockSpec((None, d, tn), lambda i, j: (i, 0, j)),
            pl.BlockSpec((None, 1, tn), lambda i, j: (i, 0, j)),
        ],
        out_specs=pl.BlockSpec((None, HALO, tn), lambda i, j: (i, 0, j)),
        out_shape=jax.ShapeDtypeStruct((depth, HALO, n), F32),
        compiler_params=_params(("parallel", "parallel")),
        name="ada_mods",
    )(cc, ada_w, ada_b.reshape(depth, 1, n))


def _qkv_kernel(ctx_ref, x_ref, sh_ref, sc_ref, g_ref, w_ref, qg_ref, kg_ref, cos_ref, sin_ref,
                gmat_ref, q_ref, kt_ref, v_ref, *, n_q, n_kv):
    t = pl.program_id(1)
    xin = jnp.where(t == 0, ctx_ref[...], x_ref[...])
    h = _rms(xin) * g_ref[...] * (1.0 + sc_ref[...]) + sh_ref[...]
    hb = h.astype(BF16)
    gmat = gmat_ref[...]
    cos = cos_ref[...]
    sin = sin_ref[...]
    lane = lax.broadcasted_iota(jnp.int32, cos.shape, 1)
    first_half = (lane % 32) < 16
    low = lane < HEAD_DIM
    nqb = n_q // LANES
    nkb = n_kv // LANES
    cw = 2 * LANES
    chunk = lambda c: jnp.dot(hb, w_ref[:, c * cw:(c + 1) * cw], preferred_element_type=F32)
    n_chunks = w_ref.shape[1] // cw
    chunks = {0: chunk(0)}
    for j in range(nqb + nkb):
        if j % 2 == 0 and j // 2 + 1 < n_chunks:
            chunks[j // 2 + 1] = chunk(j // 2 + 1)
        xb = chunks[j // 2][:, (j % 2) * LANES:(j % 2 + 1) * LANES]
        ms = _group_sum_block(xb * xb, gmat) * (1.0 / HEAD_DIM)
        gain = qg_ref[...] if j < nqb else kg_ref[...]
        xn = xb * lax.rsqrt(ms + NORM_EPS) * gain
        partner = jnp.where(first_half, pltpu.roll(xn, LANES - 16, 1), pltpu.roll(xn, 16, 1))
        xr = xn * cos + partner * sin
        if j < nqb:
            q_ref[:, j * LANES:(j + 1) * LANES] = (xr * Q_SCALE).astype(BF16)
        else:
            kt = xr.T
            jj = j - nqb
            for i in range(kt.shape[1] // ATT_BLOCK):
                kb = kt[:, i * ATT_BLOCK:(i + 1) * ATT_BLOCK]
                ke = kb[:HEAD_DIM]
                ko = kb[HEAD_DIM:]
                kt_ref[2 * jj, i] = jnp.concatenate([ke, ke], axis=0).astype(BF16)
                kt_ref[2 * jj + 1, i] = jnp.concatenate([ko, ko], axis=0).astype(BF16)
    for j in range(nkb):
        jv = nqb + nkb + j
        vb = chunks[jv // 2][:, (jv % 2) * LANES:(jv % 2 + 1) * LANES]
        sw = pltpu.roll(vb, HEAD_DIM, 1)
        v_ref[2 * j] = jnp.where(low, vb, sw).astype(BF16)
        v_ref[2 * j + 1] = jnp.where(low, sw, vb).astype(BF16)


def _mod_spec(layer, slot, n_batch, ctx_tiles, d):
    base = (layer * 6 + slot) * HALO

    def idx(b, t):
        return (base + jnp.where(t < ctx_tiles, n_batch, b), 0, 0)

    return pl.BlockSpec((None, 1, d), idx)


def _qkv_call(x, ctx, modsr, g, w_qkv, qg, kg, cos, sin, gmat, n_q, n_kv):
    bsz, s, d = x.shape
    l = ctx.shape[1]
    tm = ROW_TILE
    assert l == tm and s % tm == 0
    nt = (l + s) // tm
    t_all = l + s
    nkvh = n_kv // HEAD_DIM
    return pl.pallas_call(
        functools.partial(_qkv_kernel, n_q=n_q, n_kv=n_kv),
        grid=(bsz, nt),
        in_specs=[
            pl.BlockSpec((None, tm, d), lambda b, t: (b, 0, 0)),
            pl.BlockSpec((None, tm, d), lambda b, t: (b, jnp.maximum(t - 1, 0), 0)),
            _mod_spec(0, 0, bsz, 1, d),
            _mod_spec(0, 1, bsz, 1, d),
            _const_spec((1, d)),
            _const_spec(w_qkv.shape),
            _const_spec((1, LANES)),
            _const_spec((1, LANES)),
            pl.BlockSpec((tm, LANES), lambda b, t: (t, 0)),
            pl.BlockSpec((tm, LANES), lambda b, t: (t, 0)),
            _const_spec(gmat.shape),
        ],
        out_specs=[
            pl.BlockSpec((None, tm, n_q), lambda b, t: (b, t, 0)),
            pl.BlockSpec((None, nkvh, tm // ATT_BLOCK, LANES, ATT_BLOCK), lambda b, t: (b, 0, t, 0, 0)),
            pl.BlockSpec((None, nkvh, tm, LANES), lambda b, t: (b, 0, t, 0)),
        ],
        out_shape=[
            jax.ShapeDtypeStruct((bsz, t_all, n_q), BF16),
            jax.ShapeDtypeStruct((bsz, nkvh, t_all // ATT_BLOCK, LANES, ATT_BLOCK), BF16),
            jax.ShapeDtypeStruct((bsz, nkvh, t_all, LANES), BF16),
        ],
        compiler_params=_params(("parallel", "arbitrary")),
        name="qkv_rope",
    )(ctx, x, modsr, modsr, g, w_qkv, qg, kg, cos, sin, gmat)


def _attn_kernel(*refs, n_blocks, ctx_blocks, n_kv_heads):
    n_win = ATT_SUB + 2
    sink_ref, q_ref, kc_ref = refs[0:3]
    kw_refs = refs[3:3 + n_win]
    vc_ref = refs[3 + n_win]
    vw_refs = refs[4 + n_win:4 + 2 * n_win]
    o_ref, s_scr, p_scr, d_scr = refs[4 + 2 * n_win:]
    bq = ATT_BLOCK
    rb = ATT_ROWS
    cb = kc_ref.shape[1]
    lc = cb * bq
    m_rows = ATT_GROUP * bq
    row = lax.broadcasted_iota(jnp.int32, (rb, bq), 0)
    col = lax.broadcasted_iota(jnp.int32, (rb, bq), 1)
    far = 4 * bq
    lane = lax.broadcasted_iota(jnp.int32, (bq, LANES), 1)
    low = lane < HEAD_DIM

    def pieces(sub):
        out = [(lambda h, i=i: kc_ref[h, i], lambda h, i=i: vc_ref[h, i * bq:(i + 1) * bq], i * bq)
               for i in range(cb)]
        out += [(lambda h, r=kw_refs[sub + i]: r[h], lambda h, r=vw_refs[sub + i]: r[h],
                 lc + i * bq) for i in range(3)]
        return out

    def thresholds(sub):
        t = pl.program_id(1) * ATT_SUB + sub
        latent = t >= ctx_blocks
        return (jnp.where(t >= ctx_blocks + 1, 0, far), jnp.where(latent, -far, far),
                jnp.where(jnp.logical_and(latent, t <= n_blocks - 2), 0, far))

    def scores(sub, h, sb):
        parts = []
        for g in range(ATT_GROUP):
            blk = 2 * h + g // 2
            qp = q_ref[sub * bq:(sub + 1) * bq, blk * LANES:(blk + 1) * LANES]
            keep = low if g % 2 == 0 else jnp.logical_not(low)
            parts.append(jnp.where(keep, qp, jnp.zeros_like(qp)))
        qs = jnp.concatenate(parts, axis=0)
        for keys, _, off in pieces(sub):
            s_scr[sb, :, off:off + bq] = jnp.dot(qs, keys(h), preferred_element_type=F32)

    def softmax_rows(thr, h, sb, i):
        thr0, thr1, thr2 = thr
        r0 = i * rb
        rows = slice(r0, r0 + rb)
        iq = row + r0 % bq
        sink = sink_ref[h * ATT_GROUP + r0 // bq] * LOG2E
        sc = s_scr[sb, rows, 0:lc]
        s0 = jnp.where((col - iq) >= thr0, s_scr[sb, rows, lc:lc + bq], NEG_INF)
        s1 = jnp.where((col - iq) >= thr1, s_scr[sb, rows, lc + bq:lc + 2 * bq], NEG_INF)
        s2 = jnp.where((iq - col) >= thr2, s_scr[sb, rows, lc + 2 * bq:lc + 3 * bq], NEG_INF)
        mx = jnp.maximum(jnp.maximum(s0, s1), s2)
        for j in range(lc // bq):
            mx = jnp.maximum(mx, sc[:, j * bq:(j + 1) * bq])
        m = jnp.maximum(jnp.max(mx, axis=-1, keepdims=True), sink)
        pc = jnp.exp2(sc - m)
        p0 = jnp.exp2(s0 - m)
        p1 = jnp.exp2(s1 - m)
        p2 = jnp.exp2(s2 - m)
        ps = p0 + p1 + p2
        for j in range(lc // bq):
            ps = ps + pc[:, j * bq:(j + 1) * bq]
        den = jnp.sum(ps, axis=-1, keepdims=True) + jnp.exp2(sink - m)
        p_scr[sb, rows, 0:lc] = pc.astype(BF16)
        p_scr[sb, rows, lc:lc + bq] = p0.astype(BF16)
        p_scr[sb, rows, lc + bq:lc + 2 * bq] = p1.astype(BF16)
        p_scr[sb, rows, lc + 2 * bq:lc + 3 * bq] = p2.astype(BF16)
        d_scr[sb, rows, :] = jnp.broadcast_to(den, (rb, LANES))

    def values(sub, h, sb):
        o = sum(jnp.dot(p_scr[sb, :, off:off + bq], vals(h), preferred_element_type=F32)
                for _, vals, off in pieces(sub))
        o = o / d_scr[sb]
        for jj in range(2):
            oe = o[(2 * jj) * bq:(2 * jj + 1) * bq]
            oo = o[(2 * jj + 1) * bq:(2 * jj + 2) * bq]
            blk = 2 * h + jj
            o_ref[sub * bq:(sub + 1) * bq, blk * LANES:(blk + 1) * LANES] = (
                jnp.where(low, oe, oo).astype(BF16))

    items = [(sub, h) for sub in range(ATT_SUB) for h in range(n_kv_heads)]
    thr = [thresholds(sub) for sub in range(ATT_SUB)]
    scores(*items[0], 0)
    for n, (sub, h) in enumerate(items):
        if n + 1 < len(items):
            scores(*items[n + 1], (n + 1) % 2)
        for i in range(m_rows // rb):
            softmax_rows(thr[sub], h, n % 2, i)
        values(sub, h, n % 2)


def _attn_call(sink, q, kt, v, ctx_len):
    bsz, t_all, n_q = q.shape
    nkvh = kt.shape[1]
    bq = ATT_BLOCK
    nb = t_all // bq
    cb = ctx_len // bq
    assert kt.shape[2:] == (nb, LANES, bq) and nb % ATT_SUB == 0
    win = lambda i: (lambda t: jnp.clip(t * ATT_SUB - 1 + i, 0, nb - 1))
    kspec = lambda f: pl.BlockSpec((None, nkvh, None, LANES, bq), lambda b, t: (b, 0, f(t), 0, 0))
    vspec = lambda f: pl.BlockSpec((None, nkvh, bq, LANES), lambda b, t: (b, 0, f(t), 0))
    n_win = ATT_SUB + 2
    rows = ATT_SUB * bq
    return pl.pallas_call(
        functools.partial(_attn_kernel, n_blocks=nb, ctx_blocks=cb, n_kv_heads=nkvh),
        grid=(bsz, nb // ATT_SUB),
        in_specs=[
            pl.BlockSpec(memory_space=pltpu.SMEM),
            pl.BlockSpec((None, rows, n_q), lambda b, t: (b, t, 0)),
            pl.BlockSpec((None, nkvh, cb, LANES, bq), lambda b, t: (b, 0, 0, 0, 0)),
        ] + [kspec(win(i)) for i in range(n_win)] + [
            pl.BlockSpec((None, nkvh, ctx_len, LANES), lambda b, t: (b, 0, 0, 0)),
        ] + [vspec(win(i)) for i in range(n_win)],
        out_specs=pl.BlockSpec((None, rows, n_q), lambda b, t: (b, t, 0)),
        out_shape=jax.ShapeDtypeStruct((bsz, t_all, n_q), BF16),
        scratch_shapes=[pltpu.VMEM((2, ATT_GROUP * bq, ctx_len + 3 * bq), F32),
                        pltpu.VMEM((2, ATT_GROUP * bq, ctx_len + 3 * bq), BF16),
                        pltpu.VMEM((2, ATT_GROUP * bq, LANES), F32)],
        compiler_params=_params(("parallel", "arbitrary")),
        name="window_attention",
    )(sink, q, kt, *([kt] * n_win), v, *([v] * n_win))


def _mlp_tail(x1, sh2, sc2, gt2, g2, w1_ref, w2_ref):
    h2 = (_rms(x1) * g2 * (1.0 + sc2) + sh2).astype(BF16)
    acc = jnp.zeros(x1.shape, F32)
    for c in range(w1_ref.shape[1] // FF_CHUNK):
        hid = jnp.dot(h2, w1_ref[:, c * FF_CHUNK:(c + 1) * FF_CHUNK], preferred_element_type=F32)
        hid = jnp.square(jnp.maximum(hid, 0.0)).astype(BF16)
        acc = acc + jnp.dot(hid, w2_ref[c * FF_CHUNK:(c + 1) * FF_CHUNK, :],
                            preferred_element_type=F32)
    return x1 + gt2 * acc


def _attn_out_mlp_kernel(o_ref, ctx_ref, x_ref, gt1_ref, sh2_ref, sc2_ref, gt2_ref, g2_ref,
                         wo_ref, w1_ref, w2_ref, out_ref):
    t = pl.program_id(1)
    x0 = jnp.where(t == 0, ctx_ref[...], x_ref[...])
    y = jnp.dot(o_ref[...], wo_ref[...], preferred_element_type=F32)
    x1 = x0 + gt1_ref[...] * y
    out_ref[...] = _mlp_tail(x1, sh2_ref[...], sc2_ref[...], gt2_ref[...], g2_ref[...],
                             w1_ref, w2_ref)


def _attn_out_mlp_call(o, x, ctx, modsr, g2, w_o, w1, w2):
    bsz, s, d = x.shape
    l = ctx.shape[1]
    tm = ROW_TILE
    nt = (l + s) // tm
    return pl.pallas_call(
        _attn_out_mlp_kernel,
        grid=(bsz, nt),
        in_specs=[
            pl.BlockSpec((None, tm, d), lambda b, t: (b, t, 0)),
            pl.BlockSpec((None, tm, d), lambda b, t: (b, 0, 0)),
            pl.BlockSpec((None, tm, d), lambda b, t: (b, jnp.maximum(t - 1, 0), 0)),
            _mod_spec(0, 2, bsz, 1, d),
            _mod_spec(0, 3, bsz, 1, d),
            _mod_spec(0, 4, bsz, 1, d),
            _mod_spec(0, 5, bsz, 1, d),
            _const_spec((1, d)),
            _const_spec(w_o.shape),
            _const_spec(w1.shape),
            _const_spec(w2.shape),
        ],
        out_specs=pl.BlockSpec((None, tm, d), lambda b, t: (b, t, 0)),
        out_shape=jax.ShapeDtypeStruct((bsz, l + s, d), F32),
        compiler_params=_params(("parallel", "arbitrary")),
        name="attn_out_mlp",
    )(o, ctx, x, modsr, modsr, modsr, modsr, g2, w_o, w1, w2)


def _rwkv_feat_kernel(x_ref, xp_ref, xn_ref, sh_ref, sc_ref, g_ref, mu_ref, wrkv_ref, w0_ref,
                      w1_ref, w2_ref, a0_ref, a1_ref, a2_ref, g1_ref, g2_ref, kk_ref, ka_ref,
                      rk_ref, gmat_ref,
                      r_out, k_out, v_out, a_out, b_out, lw0_out, lw1_out, bonus_out, gate0_out,
                      gate1_out, *, n_tiles, ctx_tiles):
    t = pl.program_id(1)
    d = x_ref.shape[1]
    tm = x_ref.shape[0]
    g = g_ref[...]
    scale = 1.0 + sc_ref[...]
    shift = sh_ref[...]

    def modulate(xv):
        return _rms(xv) * g * scale + shift

    h = modulate(x_ref[...])
    first = jnp.logical_or(t == 0, t == ctx_tiles)
    last = jnp.logical_or(t == ctx_tiles - 1, t == n_tiles - 1)
    prev_row = jnp.where(first, 0.0, modulate(xp_ref[...])[HALO - 1:HALO])
    next_row = jnp.where(last, 0.0, modulate(xn_ref[...])[0:1])
    rows = lax.broadcasted_iota(jnp.int32, (tm, d), 0)
    h_prev = jnp.where(rows == 0, prev_row, pltpu.roll(h, 1, 0))
    h_next = jnp.where(rows == tm - 1, next_row, pltpu.roll(h, tm - 1, 0))
    xx = 0.5 * (h_prev + h_next) - h

    def mix(m):
        return (h + xx * mu_ref[m:m + 1, :]).astype(BF16)

    gmat = gmat_ref[...]
    xr, xk, xv = mix(0), mix(2), mix(3)
    dw = jnp.tanh(jnp.dot(mix(1), w1_ref[...], preferred_element_type=F32)).astype(BF16)
    al = jnp.dot(mix(4), a1_ref[...], preferred_element_type=F32).astype(BF16)
    gg = jax.nn.sigmoid(jnp.dot(mix(5), g1_ref[...], preferred_element_type=F32)).astype(BF16)
    cw = FEAT_COLS
    n_chunks = d // cw

    def project(j):
        cs = slice(j * cw, (j + 1) * cw)
        cs1 = slice(d + j * cw, d + (j + 1) * cw)
        dot = lambda a, w: jnp.dot(a, w, preferred_element_type=F32)
        return dict(
            r=dot(xr, wrkv_ref[0, :, cs]), k=dot(xk, wrkv_ref[1, :, cs]), v=dot(xv, wrkv_ref[2, :, cs]),
            z0=dot(dw, w2_ref[:, cs]), z1=dot(dw, w2_ref[:, cs1]), a=dot(al, a2_ref[:, cs]),
            g0=dot(gg, g2_ref[:, cs]), g1=dot(gg, g2_ref[:, cs1]))

    def finish(j, p):
        cs = slice(j * cw, (j + 1) * cw)
        cs1 = slice(d + j * cw, d + (j + 1) * cw)
        lw0_out[:, cs] = -jax.nn.sigmoid(w0_ref[:, cs] + p["z0"]) * math.exp(-0.5)
        lw1_out[:, cs] = -jax.nn.sigmoid(w0_ref[:, cs1] + p["z1"]) * math.exp(-0.5)
        gate0_out[:, cs] = p["g0"]
        gate1_out[:, cs] = p["g1"]
        alr = jax.nn.sigmoid(a0_ref[:, cs] + p["a"])
        r, k, v = p["r"], p["k"], p["v"]
        kk = k * kk_ref[:, cs]
        nrm = jnp.sqrt(_group_sum(kk * kk, gmat))
        kk = kk / jnp.maximum(nrm, 1e-12)
        k2 = k * (1.0 + (alr - 1.0) * ka_ref[:, cs])
        r_out[:, cs] = r
        k_out[:, cs] = k2
        v_out[:, cs] = v
        a_out[:, cs] = -kk
        b_out[:, cs] = kk * alr
        bonus_out[:, cs] = _group_sum(r * k2 * rk_ref[:, cs], gmat) * v

    pending = project(0)
    for j in range(1, n_chunks):
        nxt = project(j)
        finish(j - 1, pending)
        pending = nxt
    finish(n_chunks - 1, pending)


def _rwkv_feat_call(xu, modsr, n_batch, ctx_len, g, mu, wrkv, w0c, w1c, w2bd, a0, a1, a2, g1c, g2bd,
                    k_k, k_a, r_k, gmat):
    bsz, t_all, d = xu.shape
    tm = ROW_TILE
    nt = t_all // tm
    ct = ctx_len // tm
    hb = tm // HALO
    tile = pl.BlockSpec((None, tm, d), lambda b, t: (b, t, 0))
    consts = [g, mu, wrkv, w0c, w1c, w2bd, a0, a1, a2, g1c, g2bd, k_k, k_a, r_k, gmat]
    return pl.pallas_call(
        functools.partial(_rwkv_feat_kernel, n_tiles=nt, ctx_tiles=ct),
        grid=(bsz, nt),
        in_specs=[
            tile,
            pl.BlockSpec((None, HALO, d), lambda b, t: (b, jnp.maximum(t * hb - 1, 0), 0)),
            pl.BlockSpec((None, HALO, d),
                         lambda b, t: (b, jnp.minimum((t + 1) * hb, t_all // HALO - 1), 0)),
            _mod_spec(1, 0, n_batch, ct, d),
            _mod_spec(1, 1, n_batch, ct, d),
        ] + [_const_spec(c.shape) for c in consts],
        out_specs=[tile] * 10,
        out_shape=[jax.ShapeDtypeStruct((bsz, t_all, d), F32)] * 10,
        compiler_params=_params(("parallel", "arbitrary")),
        name="rwkv_features",
    )(xu, xu, xu, modsr, modsr, *consts)


def _scan_kernel(*refs):
    ins = (refs[0:6], refs[6:12])
    y_refs = refs[12:14]
    s_ref = refs[14]

    @pl.when(pl.program_id(1) == 0)
    def _():
        s_ref[...] = jnp.zeros(s_ref.shape, F32)

    c = SCAN_CHUNK
    d = y_refs[0].shape[1]
    n_sub = y_refs[0].shape[0] // c
    gw = SCAN_GROUP * HEAD_DIM
    m = SCAN_GROUP * c
    n_groups = d // gw
    rowd = lax.broadcasted_iota(jnp.int32, (c, d), 0)
    row = lax.broadcasted_iota(jnp.int32, (c, m), 0)
    lane = lax.broadcasted_iota(jnp.int32, (c, m), 1)
    lt = lane % c
    eye = jnp.where(row == lt, 1.0, 0.0)
    lane_head = lane // HEAD_DIM

    def expand(x):
        return jnp.concatenate(
            [jnp.where(lane_head == h, x, 0.0) for h in range(SCAN_GROUP)], axis=0)

    def bf(x):
        return x.astype(BF16)

    def mm(x, y):
        return jnp.dot(x, y, preferred_element_type=F32)

    def mm_nt(x, y):
        return lax.dot_general(x, y, (((1,), (1,)), ((), ())), preferred_element_type=F32)

    waves = [[] for _ in range(n_sub)]
    for direction in range(2):
        r_ref, k_ref, v_ref, a_ref, b_ref, lw_ref = ins[direction]
        reverse = direction == 1
        if reverse:
            incl, strict = row <= lt, row < lt
        else:
            incl, strict = row >= lt, row > lt
        for w in range(n_sub):
            sub = n_sub - 1 - w if reverse else w
            rows = slice(sub * c, (sub + 1) * c)
            lw = lw_ref[rows, :]
            cs = lw
            step = 1
            while step < c:
                if reverse:
                    moved = jnp.where(rowd < c - step, pltpu.roll(cs, c - step, 0), 0.0)
                else:
                    moved = jnp.where(rowd >= step, pltpu.roll(cs, step, 0), 0.0)
                cs = cs + moved
                step *= 2
            tot = cs[0:1] if reverse else cs[c - 1:c]
            e_pos = jnp.exp(cs)
            e_neg = jnp.exp(-cs)
            e_rem = jnp.exp(tot - cs)
            e_tot = jnp.exp(tot)
            rv, kv, vv, av, bv = (r_ref[rows, :], k_ref[rows, :], v_ref[rows, :], a_ref[rows, :],
                                  b_ref[rows, :])
            a_t = av * jnp.exp(cs - lw)
            r_t = rv * e_pos
            b_t = bv * e_neg
            k_t = kv * e_neg
            b_p = bv * e_rem
            k_p = kv * e_rem
            for g in range(n_groups):
                sl = slice(g * gw, (g + 1) * gw)
                waves[w].append(dict(
                    rows=rows, sl=sl, si=(direction, g), y_ref=y_refs[direction], incl=incl,
                    strict=strict,
                    xa=bf(jnp.concatenate([a_t[:, sl], r_t[:, sl]], axis=0)),
                    wb=bf(jnp.concatenate([expand(b_t[:, sl]), expand(k_t[:, sl])], axis=0)),
                    v=vv[:, sl], b_p=b_p[:, sl], k_p=k_p[:, sl], e_tot=e_tot[:, sl]))
    jobs = [j for wave in waves for j in wave]
    for j in jobs:
        j["vmb"] = bf(expand(j["v"]))
        j["big"] = mm_nt(j["xa"], j["wb"])
    for j in jobs:
        big = j.pop("big")
        j["pw"] = jnp.where(j["strict"], big[:c, :m], 0.0)
        j["a_ak"] = bf(jnp.where(j["strict"], big[:c, m:], 0.0))
        j["a_rb"] = bf(jnp.where(j["incl"], big[c:, :m], 0.0))
        j["a_rk"] = bf(jnp.where(j["incl"], big[c:, m:], 0.0))
    for j in jobs:
        j["av"] = mm(jnp.concatenate([j["a_ak"], j["a_rk"]], axis=0), j["vmb"])
        j["tinv"] = eye + j["pw"]
    for j in jobs:
        j["pw"] = mm(bf(j["pw"]), bf(expand(j["pw"])))
    span = 2
    while 2 * span < c:
        for j in jobs:
            nxt = mm(bf(jnp.concatenate([j["pw"], j["tinv"]], axis=0)), bf(expand(j["pw"])))
            j["pw"] = nxt[:c]
            j["tinv"] = j["tinv"] + nxt[c:]
        span *= 2
    for j in jobs:
        j["tinv"] = bf(j["tinv"] + mm(bf(j["tinv"]), bf(expand(j["pw"]))))
    same_head = (lax.broadcasted_iota(jnp.int32, (gw, gw), 0) // HEAD_DIM
                 == lax.broadcasted_iota(jnp.int32, (gw, gw), 1) // HEAD_DIM)
    for wave in waves:
        for j in wave:
            j["xs0"] = mm_nt(j["xa"], bf(s_ref[j["si"]]))
        for j in wave:
            j["u"] = mm(j["tinv"], bf(expand(j["xs0"][:c] + j["av"][:c])))
        for j in wave:
            j["y_ref"][j["rows"], j["sl"]] = (j["xs0"][c:] + j["av"][c:]
                                              + mm(j["a_rb"], bf(expand(j["u"]))))
        for j in wave:
            uv_t = jnp.concatenate([j["u"], j["v"]], axis=0).T
            upd = mm(bf(uv_t), bf(jnp.concatenate([j["b_p"], j["k_p"]], axis=0)))
            s_ref[j["si"]] = s_ref[j["si"]] * j["e_tot"] + jnp.where(same_head, upd, 0.0)


def _scan_call(r, k, v, a, b, lw_f, lw_b, ctx_len):
    bsz, t_all, d = r.shape
    c = SCAN_CHUNK
    assert c == HEAD_DIM
    rows = SCAN_SUB * c
    assert ctx_len % rows == 0 and t_all % rows == 0
    gw = SCAN_GROUP * HEAD_DIM
    nc = t_all // rows
    cc = ctx_len // rows
    back = lambda i: jnp.where(i < cc, cc - 1 - i, nc - 1 - (i - cc))
    fspec = pl.BlockSpec((None, rows, d), lambda bb, i: (bb, i, 0))
    bspec = pl.BlockSpec((None, rows, d), lambda bb, i: (bb, back(i), 0))
    out = jax.ShapeDtypeStruct((bsz, t_all, d), F32)
    return pl.pallas_call(
        _scan_kernel,
        grid=(bsz, nc),
        in_specs=[fspec] * 6 + [bspec] * 6,
        out_specs=[fspec, bspec],
        out_shape=[out, out],
        scratch_shapes=[pltpu.VMEM((2, d // gw, gw, gw), F32)],
        compiler_params=_params(("parallel", "arbitrary")),
        name="wkv_scan",
    )(r, k, v, a, b, lw_f, r, k, v, a, b, lw_b)


def _group_norm(y, gmat, g, b):
    mean = _group_sum(y, gmat) * (1.0 / HEAD_DIM)
    yc = y - mean
    var = _group_sum(yc * yc, gmat) * (1.0 / HEAD_DIM)
    return yc * lax.rsqrt(var + GN_EPS) * g + b


def _rwkv_out_mlp_kernel(yf_ref, yb_ref, bonus_ref, g0_ref, g1_ref, x_ref, gt1_ref, sh2_ref,
                         sc2_ref, gt2_ref, gng_ref, gnb_ref, g2_ref, gmat_ref, wo_ref, w1_ref,
                         w2_ref, out_ref):
    gmat = gmat_ref[...]
    gng, gnb = gng_ref[...], gnb_ref[...]
    bonus = bonus_ref[...]
    o = ((_group_norm(yf_ref[...], gmat, gng, gnb) + bonus) * g0_ref[...]
         + (_group_norm(yb_ref[...], gmat, gng, gnb) + bonus) * g1_ref[...])
    y = jnp.dot(o.astype(BF16), wo_ref[...], preferred_element_type=F32)
    x1 = x_ref[...] + gt1_ref[...] * y
    out_ref[...] = _mlp_tail(x1, sh2_ref[...], sc2_ref[...], gt2_ref[...], g2_ref[...],
                             w1_ref, w2_ref)


def _rwkv_out_mlp_call(yf, yb, bonus, g0, g1, xu, modsr, ctx_len, gng, gnb, g2, gmat, w_o, w1, w2):
    bsz, t_all, d = xu.shape
    tm = ROW_TILE
    ct = ctx_len // tm
    ns = (t_all - ctx_len) // tm
    tile = pl.BlockSpec((None, tm, d), lambda b, t: (b, t + ct, 0))
    lat_mod = lambda slot: pl.BlockSpec((None, 1, d), lambda b, t: ((6 + slot) * HALO + b, 0, 0))
    consts = [gng, gnb, g2, gmat, w_o, w1, w2]
    return pl.pallas_call(
        _rwkv_out_mlp_kernel,
        grid=(bsz, ns),
        in_specs=[tile] * 6 + [lat_mod(2), lat_mod(3), lat_mod(4), lat_mod(5)]
        + [_const_spec(c.shape) for c in consts],
        out_specs=pl.BlockSpec((None, tm, d), lambda b, t: (b, t, 0)),
        out_shape=jax.ShapeDtypeStruct((bsz, t_all - ctx_len, d), F32),
        compiler_params=_params(("parallel", "arbitrary")),
        name="rwkv_out_mlp",
    )(yf, yb, bonus, g0, g1, xu, modsr, modsr, modsr, modsr, *consts)


def _rope_tables(ctx_len, s):
    half = HEAD_DIM // 4
    lane = jnp.arange(HEAD_DIM)
    freqs = ROPE_BASE ** (-(lane % half).astype(F32) / half)
    pos = jnp.arange(s)
    coord = jnp.where(lane[None, :] < HEAD_DIM // 2, (pos // GRID_W)[:, None], (pos % GRID_W)[:, None])
    ang = coord.astype(F32) * freqs[None, :]
    sign = jnp.where((lane % (2 * half)) < half, -1.0, 1.0)
    cos = jnp.concatenate([jnp.ones((ctx_len, HEAD_DIM), F32), jnp.cos(ang)], axis=0)
    sin = jnp.concatenate([jnp.zeros((ctx_len, HEAD_DIM), F32), jnp.sin(ang) * sign], axis=0)
    return jnp.tile(cos, (1, 2)), jnp.tile(sin, (1, 2))


def _block_diag2(w):
    z = jnp.zeros_like(w[0])
    return jnp.concatenate([jnp.concatenate([w[0], z], axis=1),
                            jnp.concatenate([z, w[1]], axis=1)], axis=0)


def kernel(x, c, ctx, c_ctx, ada_w, ada_b, norm1_g, norm2_g, mlp_w1, mlp_w2, att_w_qkv, att_q_gain, att_k_gain, att_sink, att_w_o, rwkv_mu, rwkv_w_rkv, rwkv_w0, rwkv_w1, rwkv_w2, rwkv_a0, rwkv_a1, rwkv_a2, rwkv_g1, rwkv_g2, rwkv_k_k, rwkv_k_a, rwkv_r_k, rwkv_gn_g, rwkv_gn_b, rwkv_w_o):
    bsz, s, d = x.shape
    l = ctx.shape[1]
    depth = ada_w.shape[0]
    assert depth == 2 and bsz < HALO and d % LANES == 0
    n_q = att_w_o.shape[1]
    n_kv = (att_w_qkv.shape[2] - n_q) // 2
    gmat = _group_matrix()

    cc = jnp.concatenate([c, c_ctx[None, :], jnp.zeros((HALO - bsz - 1, d), F32)], axis=0)
    mods = _ada_mods(cc, ada_w, ada_b)
    modsr = mods.reshape(depth, HALO, 6, d).transpose(0, 2, 1, 3).reshape(depth * 6 * HALO, 1, d)

    cos, sin = _rope_tables(l, s)
    q, kt, v = _qkv_call(x, ctx, modsr, norm1_g[0][None], att_w_qkv[0].astype(BF16),
                         jnp.tile(att_q_gain[0], 2)[None], jnp.tile(att_k_gain[0], 2)[None],
                         cos, sin, gmat, n_q, n_kv)
    o = _attn_call(att_sink[0], q, kt, v, l)
    xu = _attn_out_mlp_call(o, x, ctx, modsr, norm2_g[0][None], att_w_o[0].astype(BF16),
                            mlp_w1[0].astype(BF16), mlp_w2[0].astype(BF16))

    row = lambda a: a.reshape(1, -1)
    w1c = jnp.concatenate([rwkv_w1[0, 0], rwkv_w1[0, 1]], axis=1).astype(BF16)
    g1c = jnp.concatenate([rwkv_g1[0, 0], rwkv_g1[0, 1]], axis=1).astype(BF16)
    feats = _rwkv_feat_call(
        xu, modsr, bsz, l, norm1_g[1][None], rwkv_mu[0], rwkv_w_rkv[0].astype(BF16),
        row(rwkv_w0[0]), w1c, _block_diag2(rwkv_w2[0]).astype(BF16), row(rwkv_a0[0]),
        rwkv_a1[0].astype(BF16), rwkv_a2[0].astype(BF16), g1c,
        _block_diag2(rwkv_g2[0]).astype(BF16), row(rwkv_k_k[0]), row(rwkv_k_a[0]),
        row(rwkv_r_k[0]), gmat)
    r, k, vv, a, b, lw0, lw1, bonus, gate0, gate1 = feats
    y_fwd, y_bwd = _scan_call(r, k, vv, a, b, lw0, lw1, l)
    return _rwkv_out_mlp_call(y_fwd, y_bwd, bonus, gate0, gate1, xu, modsr, l, row(rwkv_gn_g[0]),
                              row(rwkv_gn_b[0]), norm2_g[1][None], gmat, rwkv_w_o[0].astype(BF16),
                              mlp_w1[1].astype(BF16), mlp_w2[1].astype(BF16))
```

```python
import functools
import math

import jax
import jax.numpy as jnp
from jax import lax
from jax.experimental import pallas as pl
from jax.experimental.pallas import tpu as pltpu

F32 = jnp.float32
BF16 = jnp.bfloat16

HEAD_DIM = 64
LANES = 128
HALO = 8
ATT_GROUP = 4
ATT_BLOCK = 128
ATT_SUB = 6
ATT_ROWS = 32
ROPE_BASE = 10000.0
GRID_W = 64
NORM_EPS = 1e-6
GN_EPS = 64e-5
NEG_INF = -1e30
LOG2E = math.log2(math.e)
Q_SCALE = HEAD_DIM ** -0.5 * LOG2E
SCAN_CHUNK = 64
SCAN_GROUP = 2
SCAN_SUB = 4
ROW_TILE = 256
FEAT_COLS = 256
FF_CHUNK = 1024
VMEM_LIMIT = 56 * 1024 * 1024


def _params(sem):
    return pltpu.CompilerParams(dimension_semantics=sem, vmem_limit_bytes=VMEM_LIMIT)


def _const_spec(shape):
    n = len(shape)
    return pl.BlockSpec(shape, lambda *_: (0,) * n, pipeline_mode=pl.Buffered(1))


def _rms(x):
    return x * lax.rsqrt(jnp.mean(x * x, axis=-1, keepdims=True) + NORM_EPS)


def _group_sum_block(xb, gmat):
    hi = xb.astype(BF16)
    lo = (xb - hi.astype(F32)).astype(BF16)
    return jnp.dot(jnp.concatenate([hi, lo], axis=1), gmat, preferred_element_type=F32)


def _group_sum(x, gmat):
    n = x.shape[1] // LANES
    return jnp.concatenate(
        [_group_sum_block(x[:, j * LANES:(j + 1) * LANES], gmat) for j in range(n)], axis=1)


def _group_matrix():
    i = jnp.arange(2 * LANES)[:, None]
    j = jnp.arange(LANES)[None, :]
    return (((i % LANES) // HEAD_DIM) == (j // HEAD_DIM)).astype(BF16)


def _ada_kernel(c_ref, w_ref, b_ref, o_ref):
    c = c_ref[...]
    s = c * jax.nn.sigmoid(c)
    o_ref[...] = jnp.dot(s, w_ref[...], precision=lax.Precision.HIGHEST,
                         preferred_element_type=F32) + b_ref[...]


def _ada_mods(cc, ada_w, ada_b):
    depth, d, n = ada_w.shape
    tn = n // 4
    return pl.pallas_call(
        _ada_kernel,
        grid=(depth, n // tn),
        in_specs=[
            pl.BlockSpec((HALO, d), lambda i, j: (0, 0)),
            pl.BlockSpec((None, d, tn), lambda i, j: (i, 0, j)),
            pl.BlockSpec((None, 1, tn), lambda i, j: (i, 0, j)),
        ],
        out_specs=pl.BlockSpec((None, HALO, tn), lambda i, j: (i, 0, j)),
        out_shape=jax.ShapeDtypeStruct((depth, HALO, n), F32),
        compiler_params=_params(("parallel", "parallel")),
        name="ada_mods",
    )(cc, ada_w, ada_b.reshape(depth, 1, n))


def _qkv_kernel(ctx_ref, x_ref, sh_ref, sc_ref, g_ref, w_ref, qg_ref, kg_ref, cos_ref, sin_ref,
                gmat_ref, q_ref, kt_ref, v_ref, *, n_q, n_kv):
    t = pl.program_id(1)
    xin = jnp.where(t == 0, ctx_ref[...], x_ref[...])
    h = _rms(xin) * g_ref[...] * (1.0 + sc_ref[...]) + sh_ref[...]
    hb = h.astype(BF16)
    gmat = gmat_ref[...]
    cos = cos_ref[...]
    sin = sin_ref[...]
    lane = lax.broadcasted_iota(jnp.int32, cos.shape, 1)
    first_half = (lane % 32) < 16
    low = lane < HEAD_DIM
    nqb = n_q // LANES
    nkb = n_kv // LANES
    cw = 2 * LANES
    chunk = lambda c: jnp.dot(hb, w_ref[:, c * cw:(c + 1) * cw], preferred_element_type=F32)
    n_chunks = w_ref.shape[1] // cw
    chunks = {0: chunk(0)}
    for j in range(nqb + nkb):
        if j % 2 == 0 and j // 2 + 1 < n_chunks:
            chunks[j // 2 + 1] = chunk(j // 2 + 1)
        xb = chunks[j // 2][:, (j % 2) * LANES:(j % 2 + 1) * LANES]
        ms = _group_sum_block(xb * xb, gmat) * (1.0 / HEAD_DIM)
        gain = qg_ref[...] if j < nqb else kg_ref[...]
        xn = xb * lax.rsqrt(ms + NORM_EPS) * gain
        partner = jnp.where(first_half, pltpu.roll(xn, LANES - 16, 1), pltpu.roll(xn, 16, 1))
        xr = xn * cos + partner * sin
        if j < nqb:
            q_ref[:, j * LANES:(j + 1) * LANES] = (xr * Q_SCALE).astype(BF16)
        else:
            kt = xr.T
            jj = j - nqb
            for i in range(kt.shape[1] // ATT_BLOCK):
                kb = kt[:, i * ATT_BLOCK:(i + 1) * ATT_BLOCK]
                ke = kb[:HEAD_DIM]
                ko = kb[HEAD_DIM:]
                kt_ref[2 * jj, i] = jnp.concatenate([ke, ke], axis=0).astype(BF16)
                kt_ref[2 * jj + 1, i] = jnp.concatenate([ko, ko], axis=0).astype(BF16)
    for j in range(nkb):
        jv = nqb + nkb + j
        vb = chunks[jv // 2][:, (jv % 2) * LANES:(jv % 2 + 1) * LANES]
        sw = pltpu.roll(vb, HEAD_DIM, 1)
        v_ref[2 * j] = jnp.where(low, vb, sw).astype(BF16)
        v_ref[2 * j + 1] = jnp.where(low, sw, vb).astype(BF16)


def _mod_spec(layer, slot, n_batch, ctx_tiles, d):
    base = (layer * 6 + slot) * HALO

    def idx(b, t):
        return (base + jnp.where(t < ctx_tiles, n_batch, b), 0, 0)

    return pl.BlockSpec((None, 1, d), idx)


def _qkv_call(x, ctx, modsr, g, w_qkv, qg, kg, cos, sin, gmat, n_q, n_kv):
    bsz, s, d = x.shape
    l = ctx.shape[1]
    tm = ROW_TILE
    assert l == tm and s % tm == 0
    nt = (l + s) // tm
    t_all = l + s
    nkvh = n_kv // HEAD_DIM
    return pl.pallas_call(
        functools.partial(_qkv_kernel, n_q=n_q, n_kv=n_kv),
        grid=(bsz, nt),
        in_specs=[
            pl.BlockSpec((None, tm, d), lambda b, t: (b, 0, 0)),
            pl.BlockSpec((None, tm, d), lambda b, t: (b, jnp.maximum(t - 1, 0), 0)),
            _mod_spec(0, 0, bsz, 1, d),
            _mod_spec(0, 1, bsz, 1, d),
            _const_spec((1, d)),
            _const_spec(w_qkv.shape),
            _const_spec((1, LANES)),
            _const_spec((1, LANES)),
            pl.BlockSpec((tm, LANES), lambda b, t: (t, 0)),
            pl.BlockSpec((tm, LANES), lambda b, t: (t, 0)),
            _const_spec(gmat.shape),
        ],
        out_specs=[
            pl.BlockSpec((None, tm, n_q), lambda b, t: (b, t, 0)),
            pl.BlockSpec((None, nkvh, tm // ATT_BLOCK, LANES, ATT_BLOCK), lambda b, t: (b, 0, t, 0, 0)),
            pl.BlockSpec((None, nkvh, tm, LANES), lambda b, t: (b, 0, t, 0)),
        ],
        out_shape=[
            jax.ShapeDtypeStruct((bsz, t_all, n_q), BF16),
            jax.ShapeDtypeStruct((bsz, nkvh, t_all // ATT_BLOCK, LANES, ATT_BLOCK), BF16),
            jax.ShapeDtypeStruct((bsz, nkvh, t_all, LANES), BF16),
        ],
        compiler_params=_params(("parallel", "arbitrary")),
        name="qkv_rope",
    )(ctx, x, modsr, modsr, g, w_qkv, qg, kg, cos, sin, gmat)


def _attn_kernel(*refs, n_blocks, ctx_blocks, n_kv_heads):
    n_win = ATT_SUB + 2
    sink_ref, q_ref, kc_ref = refs[0:3]
    kw_refs = refs[3:3 + n_win]
    vc_ref = refs[3 + n_win]
    vw_refs = refs[4 + n_win:4 + 2 * n_win]
    o_ref, s_scr, p_scr, d_scr = refs[4 + 2 * n_win:]
    bq = ATT_BLOCK
    rb = ATT_ROWS
    cb = kc_ref.shape[1]
    lc = cb * bq
    m_rows = ATT_GROUP * bq
    row = lax.broadcasted_iota(jnp.int32, (rb, bq), 0)
    col = lax.broadcasted_iota(jnp.int32, (rb, bq), 1)
    far = 4 * bq
    lane = lax.broadcasted_iota(jnp.int32, (bq, LANES), 1)
    low = lane < HEAD_DIM

    def pieces(sub):
        out = [(lambda h, i=i: kc_ref[h, i], lambda h, i=i: vc_ref[h, i * bq:(i + 1) * bq], i * bq)
               for i in range(cb)]
        out += [(lambda h, r=kw_refs[sub + i]: r[h], lambda h, r=vw_refs[sub + i]: r[h],
                 lc + i * bq) for i in range(3)]
        return out

    def thresholds(sub):
        t = pl.program_id(1) * ATT_SUB + sub
        latent = t >= ctx_blocks
        return (jnp.where(t >= ctx_blocks + 1, 0, far), jnp.where(latent, -far, far),
                jnp.where(jnp.logical_and(latent, t <= n_blocks - 2), 0, far))

    def scores(sub, h, sb):
        parts = []
        for g in range(ATT_GROUP):
            blk = 2 * h + g // 2
            qp = q_ref[sub * bq:(sub + 1) * bq, blk * LANES:(blk + 1) * LANES]
            keep = low if g % 2 == 0 else jnp.logical_not(low)
            parts.append(jnp.where(keep, qp, jnp.zeros_like(qp)))
        qs = jnp.concatenate(parts, axis=0)
        for keys, _, off in pieces(sub):
            s_scr[sb, :, off:off + bq] = jnp.dot(qs, keys(h), preferred_element_type=F32)

    def softmax_rows(thr, h, sb, i):
        thr0, thr1, thr2 = thr
        r0 = i * rb
        rows = slice(r0, r0 + rb)
        iq = row + r0 % bq
        sink = sink_ref[h * ATT_GROUP + r0 // bq] * LOG2E
        sc = s_scr[sb, rows, 0:lc]
        s0 = jnp.where((col - iq) >= thr0, s_scr[sb, rows, lc:lc + bq], NEG_INF)
        s1 = jnp.where((col - iq) >= thr1, s_scr[sb, rows, lc + bq:lc + 2 * bq], NEG_INF)
        s2 = jnp.where((iq - col) >= thr2, s_scr[sb, rows, lc + 2 * bq:lc + 3 * bq], NEG_INF)
        mx = jnp.maximum(jnp.maximum(s0, s1), s2)
        for j in range(lc // bq):
            mx = jnp.maximum(mx, sc[:, j * bq:(j + 1) * bq])
        m = jnp.maximum(jnp.max(mx, axis=-1, keepdims=True), sink)
        pc = jnp.exp2(sc - m)
        p0 = jnp.exp2(s0 - m)
        p1 = jnp.exp2(s1 - m)
        p2 = jnp.exp2(s2 - m)
        ps = p0 + p1 + p2
        for j in range(lc // bq):
            ps = ps + pc[:, j * bq:(j + 1) * bq]
        den = jnp.sum(ps, axis=-1, keepdims=True) + jnp.exp2(sink - m)
        p_scr[sb, rows, 0:lc] = pc.astype(BF16)
        p_scr[sb, rows, lc:lc + bq] = p0.astype(BF16)
        p_scr[sb, rows, lc + bq:lc + 2 * bq] = p1.astype(BF16)
        p_scr[sb, rows, lc + 2 * bq:lc + 3 * bq] = p2.astype(BF16)
        d_scr[sb, rows, :] = jnp.broadcast_to(den, (rb, LANES))

    def values(sub, h, sb):
        o = sum(jnp.dot(p_scr[sb, :, off:off + bq], vals(h), preferred_element_type=F32)
                for _, vals, off in pieces(sub))
        o = o / d_scr[sb]
        for jj in range(2):
            oe = o[(2 * jj) * bq:(2 * jj + 1) * bq]
            oo = o[(2 * jj + 1) * bq:(2 * jj + 2) * bq]
            blk = 2 * h + jj
            o_ref[sub * bq:(sub + 1) * bq, blk * LANES:(blk + 1) * LANES] = (
                jnp.where(low, oe, oo).astype(BF16))

    items = [(sub, h) for sub in range(ATT_SUB) for h in range(n_kv_heads)]
    thr = [thresholds(sub) for sub in range(ATT_SUB)]
    scores(*items[0], 0)
    for n, (sub, h) in enumerate(items):
        if n + 1 < len(items):
            scores(*items[n + 1], (n + 1) % 2)
        for i in range(m_rows // rb):
            softmax_rows(thr[sub], h, n % 2, i)
        values(sub, h, n % 2)


def _attn_call(sink, q, kt, v, ctx_len):
    bsz, t_all, n_q = q.shape
    nkvh = kt.shape[1]
    bq = ATT_BLOCK
    nb = t_all // bq
    cb = ctx_len // bq
    assert kt.shape[2:] == (nb, LANES, bq) and nb % ATT_SUB == 0
    win = lambda i: (lambda t: jnp.clip(t * ATT_SUB - 1 + i, 0, nb - 1))
    kspec = lambda f: pl.BlockSpec((None, nkvh, None, LANES, bq), lambda b, t: (b, 0, f(t), 0, 0))
    vspec = lambda f: pl.BlockSpec((None, nkvh, bq, LANES), lambda b, t: (b, 0, f(t), 0))
    n_win = ATT_SUB + 2
    rows = ATT_SUB * bq
    return pl.pallas_call(
        functools.partial(_attn_kernel, n_blocks=nb, ctx_blocks=cb, n_kv_heads=nkvh),
        grid=(bsz, nb // ATT_SUB),
        in_specs=[
            pl.BlockSpec(memory_space=pltpu.SMEM),
            pl.BlockSpec((None, rows, n_q), lambda b, t: (b, t, 0)),
            pl.BlockSpec((None, nkvh, cb, LANES, bq), lambda b, t: (b, 0, 0, 0, 0)),
        ] + [kspec(win(i)) for i in range(n_win)] + [
            pl.BlockSpec((None, nkvh, ctx_len, LANES), lambda b, t: (b, 0, 0, 0)),
        ] + [vspec(win(i)) for i in range(n_win)],
        out_specs=pl.BlockSpec((None, rows, n_q), lambda b, t: (b, t, 0)),
        out_shape=jax.ShapeDtypeStruct((bsz, t_all, n_q), BF16),
        scratch_shapes=[pltpu.VMEM((2, ATT_GROUP * bq, ctx_len + 3 * bq), F32),
                        pltpu.VMEM((2, ATT_GROUP * bq, ctx_len + 3 * bq), BF16),
                        pltpu.VMEM((2, ATT_GROUP * bq, LANES), F32)],
        compiler_params=_params(("parallel", "arbitrary")),
        name="window_attention",
    )(sink, q, kt, *([kt] * n_win), v, *([v] * n_win))


def _mlp_tail(x1, sh2, sc2, gt2, g2, w1_ref, w2_ref):
    h2 = (_rms(x1) * g2 * (1.0 + sc2) + sh2).astype(BF16)
    acc = jnp.zeros(x1.shape, F32)
    for c in range(w1_ref.shape[1] // FF_CHUNK):
        hid = jnp.dot(h2, w1_ref[:, c * FF_CHUNK:(c + 1) * FF_CHUNK], preferred_element_type=F32)
        hid = jnp.square(jnp.maximum(hid, 0.0)).astype(BF16)
        acc = acc + jnp.dot(hid, w2_ref[c * FF_CHUNK:(c + 1) * FF_CHUNK, :],
                            preferred_element_type=F32)
    return x1 + gt2 * acc


def _attn_out_mlp_kernel(o_ref, ctx_ref, x_ref, gt1_ref, sh2_ref, sc2_ref, gt2_ref, g2_ref,
                         wo_ref, w1_ref, w2_ref, out_ref):
    t = pl.program_id(1)
    x0 = jnp.where(t == 0, ctx_ref[...], x_ref[...])
    y = jnp.dot(o_ref[...], wo_ref[...], preferred_element_type=F32)
    x1 = x0 + gt1_ref[...] * y
    out_ref[...] = _mlp_tail(x1, sh2_ref[...], sc2_ref[...], gt2_ref[...], g2_ref[...],
                             w1_ref, w2_ref)


def _attn_out_mlp_call(o, x, ctx, modsr, g2, w_o, w1, w2):
    bsz, s, d = x.shape
    l = ctx.shape[1]
    tm = ROW_TILE
    nt = (l + s) // tm
    return pl.pallas_call(
        _attn_out_mlp_kernel,
        grid=(bsz, nt),
        in_specs=[
            pl.BlockSpec((None, tm, d), lambda b, t: (b, t, 0)),
            pl.BlockSpec((None, tm, d), lambda b, t: (b, 0, 0)),
            pl.BlockSpec((None, tm, d), lambda b, t: (b, jnp.maximum(t - 1, 0), 0)),
            _mod_spec(0, 2, bsz, 1, d),
            _mod_spec(0, 3, bsz, 1, d),
            _mod_spec(0, 4, bsz, 1, d),
            _mod_spec(0, 5, bsz, 1, d),
            _const_spec((1, d)),
            _const_spec(w_o.shape),
            _const_spec(w1.shape),
            _const_spec(w2.shape),
        ],
        out_specs=pl.BlockSpec((None, tm, d), lambda b, t: (b, t, 0)),
        out_shape=jax.ShapeDtypeStruct((bsz, l + s, d), F32),
        compiler_params=_params(("parallel", "arbitrary")),
        name="attn_out_mlp",
    )(o, ctx, x, modsr, modsr, modsr, modsr, g2, w_o, w1, w2)


def _rwkv_feat_kernel(x_ref, xp_ref, xn_ref, sh_ref, sc_ref, g_ref, mu_ref, wrkv_ref, w0_ref,
                      w1_ref, w2_ref, a0_ref, a1_ref, a2_ref, g1_ref, g2_ref, kk_ref, ka_ref,
                      rk_ref, gmat_ref,
                      pk_out, lw0_out, lw1_out, bonus_out, gate0_out,
                      gate1_out, *, n_tiles, ctx_tiles):
    t = pl.program_id(1)
    d = x_ref.shape[1]
    tm = x_ref.shape[0]
    g = g_ref[...]
    scale = 1.0 + sc_ref[...]
    shift = sh_ref[...]

    def modulate(xv):
        return _rms(xv) * g * scale + shift

    h = modulate(x_ref[...])
    first = jnp.logical_or(t == 0, t == ctx_tiles)
    last = jnp.logical_or(t == ctx_tiles - 1, t == n_tiles - 1)
    prev_row = jnp.where(first, 0.0, modulate(xp_ref[...])[HALO - 1:HALO])
    next_row = jnp.where(last, 0.0, modulate(xn_ref[...])[0:1])
    rows = lax.broadcasted_iota(jnp.int32, (tm, d), 0)
    h_prev = jnp.where(rows == 0, prev_row, pltpu.roll(h, 1, 0))
    h_next = jnp.where(rows == tm - 1, next_row, pltpu.roll(h, tm - 1, 0))
    xx = 0.5 * (h_prev + h_next) - h

    def mix(m):
        return (h + xx * mu_ref[m:m + 1, :]).astype(BF16)

    gmat = gmat_ref[...]
    xr, xk, xv = mix(0), mix(2), mix(3)
    dw = jnp.tanh(jnp.dot(mix(1), w1_ref[...], preferred_element_type=F32)).astype(BF16)
    al = jnp.dot(mix(4), a1_ref[...], preferred_element_type=F32).astype(BF16)
    gg = jax.nn.sigmoid(jnp.dot(mix(5), g1_ref[...], preferred_element_type=F32)).astype(BF16)
    cw = FEAT_COLS
    n_chunks = d // cw

    def project(j):
        cs = slice(j * cw, (j + 1) * cw)
        cs1 = slice(d + j * cw, d + (j + 1) * cw)
        dot = lambda a, w: jnp.dot(a, w, preferred_element_type=F32)
        return dict(
            r=dot(xr, wrkv_ref[0, :, cs]), k=dot(xk, wrkv_ref[1, :, cs]), v=dot(xv, wrkv_ref[2, :, cs]),
            z0=dot(dw, w2_ref[:, cs]), z1=dot(dw, w2_ref[:, cs1]), a=dot(al, a2_ref[:, cs]),
            g0=dot(gg, g2_ref[:, cs]), g1=dot(gg, g2_ref[:, cs1]))

    def finish(j, p):
        cs = slice(j * cw, (j + 1) * cw)
        cs1 = slice(d + j * cw, d + (j + 1) * cw)
        lw0_out[:, cs] = -jax.nn.sigmoid(w0_ref[:, cs] + p["z0"]) * math.exp(-0.5)
        lw1_out[:, cs] = -jax.nn.sigmoid(w0_ref[:, cs1] + p["z1"]) * math.exp(-0.5)
        gate0_out[:, cs] = p["g0"]
        gate1_out[:, cs] = p["g1"]
        alr = jax.nn.sigmoid(a0_ref[:, cs] + p["a"])
        r, k, v = p["r"], p["k"], p["v"]
        kk = k * kk_ref[:, cs]
        nrm = jnp.sqrt(_group_sum(kk * kk, gmat))
        kk = kk / jnp.maximum(nrm, 1e-12)
        k2 = k * (1.0 + (alr - 1.0) * ka_ref[:, cs])
        for i, val in enumerate((r, k2, v, -kk, kk * alr)):
            pk_out[:, i * d + j * cw:i * d + (j + 1) * cw] = val
        bonus_out[:, cs] = _group_sum(r * k2 * rk_ref[:, cs], gmat) * v

    pending = project(0)
    for j in range(1, n_chunks):
        nxt = project(j)
        finish(j - 1, pending)
        pending = nxt
    finish(n_chunks - 1, pending)


def _rwkv_feat_call(xu, modsr, n_batch, ctx_len, g, mu, wrkv, w0c, w1c, w2bd, a0, a1, a2, g1c, g2bd,
                    k_k, k_a, r_k, gmat):
    bsz, t_all, d = xu.shape
    tm = ROW_TILE
    nt = t_all // tm
    ct = ctx_len // tm
    hb = tm // HALO
    tile = pl.BlockSpec((None, tm, d), lambda b, t: (b, t, 0))
    consts = [g, mu, wrkv, w0c, w1c, w2bd, a0, a1, a2, g1c, g2bd, k_k, k_a, r_k, gmat]
    return pl.pallas_call(
        functools.partial(_rwkv_feat_kernel, n_tiles=nt, ctx_tiles=ct),
        grid=(bsz, nt),
        in_specs=[
            tile,
            pl.BlockSpec((None, HALO, d), lambda b, t: (b, jnp.maximum(t * hb - 1, 0), 0)),
            pl.BlockSpec((None, HALO, d),
                         lambda b, t: (b, jnp.minimum((t + 1) * hb, t_all // HALO - 1), 0)),
            _mod_spec(1, 0, n_batch, ct, d),
            _mod_spec(1, 1, n_batch, ct, d),
        ] + [_const_spec(c.shape) for c in consts],
        out_specs=[pl.BlockSpec((None, tm, 5 * d), lambda b, t: (b, t, 0))] + [tile] * 5,
        out_shape=[jax.ShapeDtypeStruct((bsz, t_all, 5 * d), F32)]
        + [jax.ShapeDtypeStruct((bsz, t_all, d), F32)] * 5,
        compiler_params=_params(("parallel", "arbitrary")),
        name="rwkv_features",
    )(xu, xu, xu, modsr, modsr, *consts)


def _scan_kernel(*refs):
    ins = (refs[0:2], refs[2:4])
    y_refs = refs[4:6]
    s_ref = refs[6]

    @pl.when(pl.program_id(1) == 0)
    def _():
        s_ref[...] = jnp.zeros(s_ref.shape, F32)

    c = SCAN_CHUNK
    d = y_refs[0].shape[1]
    n_sub = y_refs[0].shape[0] // c
    gw = SCAN_GROUP * HEAD_DIM
    m = SCAN_GROUP * c
    n_groups = d // gw
    rowd = lax.broadcasted_iota(jnp.int32, (c, d), 0)
    row = lax.broadcasted_iota(jnp.int32, (c, m), 0)
    lane = lax.broadcasted_iota(jnp.int32, (c, m), 1)
    lt = lane % c
    eye = jnp.where(row == lt, 1.0, 0.0)
    lane_head = lane // HEAD_DIM

    def expand(x):
        return jnp.concatenate(
            [jnp.where(lane_head == h, x, 0.0) for h in range(SCAN_GROUP)], axis=0)

    def bf(x):
        return x.astype(BF16)

    def mm(x, y):
        return jnp.dot(x, y, preferred_element_type=F32)

    def mm_nt(x, y):
        return lax.dot_general(x, y, (((1,), (1,)), ((), ())), preferred_element_type=F32)

    waves = [[] for _ in range(n_sub)]
    for direction in range(2):
        pk_ref, lw_ref = ins[direction]
        reverse = direction == 1
        if reverse:
            incl, strict = row <= lt, row < lt
        else:
            incl, strict = row >= lt, row > lt
        for w in range(n_sub):
            sub = n_sub - 1 - w if reverse else w
            rows = slice(sub * c, (sub + 1) * c)
            lw = lw_ref[rows, :]
            cs = lw
            step = 1
            while step < c:
                if reverse:
                    moved = jnp.where(rowd < c - step, pltpu.roll(cs, c - step, 0), 0.0)
                else:
                    moved = jnp.where(rowd >= step, pltpu.roll(cs, step, 0), 0.0)
                cs = cs + moved
                step *= 2
            tot = cs[0:1] if reverse else cs[c - 1:c]
            e_pos = jnp.exp(cs)
            e_neg = jnp.exp(-cs)
            e_rem = jnp.exp(tot - cs)
            e_tot = jnp.exp(tot)
            rv, kv, vv, av, bv = (pk_ref[rows, i * d:(i + 1) * d] for i in range(5))
            a_t = av * jnp.exp(cs - lw)
            r_t = rv * e_pos
            b_t = bv * e_neg
            k_t = kv * e_neg
            b_p = bv * e_rem
            k_p = kv * e_rem
            for g in range(n_groups):
                sl = slice(g * gw, (g + 1) * gw)
                waves[w].append(dict(
                    rows=rows, sl=sl, si=(direction, g), y_ref=y_refs[direction], incl=incl,
                    strict=strict,
                    xa=bf(jnp.concatenate([a_t[:, sl], r_t[:, sl]], axis=0)),
                    wb=bf(jnp.concatenate([expand(b_t[:, sl]), expand(k_t[:, sl])], axis=0)),
                    v=vv[:, sl], b_p=b_p[:, sl], k_p=k_p[:, sl], e_tot=e_tot[:, sl]))
    jobs = [j for wave in waves for j in wave]
    for j in jobs:
        j["vmb"] = bf(expand(j["v"]))
        j["big"] = mm_nt(j["xa"], j["wb"])
    for j in jobs:
        big = j.pop("big")
        j["pw"] = jnp.where(j["strict"], big[:c, :m], 0.0)
        j["a_ak"] = bf(jnp.where(j["strict"], big[:c, m:], 0.0))
        j["a_rb"] = bf(jnp.where(j["incl"], big[c:, :m], 0.0))
        j["a_rk"] = bf(jnp.where(j["incl"], big[c:, m:], 0.0))
    for j in jobs:
        j["av"] = mm(jnp.concatenate([j["a_ak"], j["a_rk"]], axis=0), j["vmb"])
        j["tinv"] = eye + j["pw"]
    for j in jobs:
        j["pw"] = mm(bf(j["pw"]), bf(expand(j["pw"])))
    span = 2
    while 2 * span < c:
        for j in jobs:
            nxt = mm(bf(jnp.concatenate([j["pw"], j["tinv"]], axis=0)), bf(expand(j["pw"])))
            j["pw"] = nxt[:c]
            j["tinv"] = j["tinv"] + nxt[c:]
        span *= 2
    for j in jobs:
        j["tinv"] = bf(j["tinv"] + mm(bf(j["tinv"]), bf(expand(j["pw"]))))
    same_head = (lax.broadcasted_iota(jnp.int32, (gw, gw), 0) // HEAD_DIM
                 == lax.broadcasted_iota(jnp.int32, (gw, gw), 1) // HEAD_DIM)
    for wave in waves:
        for j in wave:
            j["xs0"] = mm_nt(j["xa"], bf(s_ref[j["si"]]))
        for j in wave:
            j["u"] = mm(j["tinv"], bf(expand(j["xs0"][:c] + j["av"][:c])))
        for j in wave:
            j["y_ref"][j["rows"], j["sl"]] = (j["xs0"][c:] + j["av"][c:]
                                              + mm(j["a_rb"], bf(expand(j["u"]))))
        for j in wave:
            uv_t = jnp.concatenate([j["u"], j["v"]], axis=0).T
            upd = mm(bf(uv_t), bf(jnp.concatenate([j["b_p"], j["k_p"]], axis=0)))
            s_ref[j["si"]] = s_ref[j["si"]] * j["e_tot"] + jnp.where(same_head, upd, 0.0)


def _scan_call(pk, lw_f, lw_b, ctx_len):
    bsz, t_all, d = lw_f.shape
    c = SCAN_CHUNK
    assert c == HEAD_DIM
    rows = SCAN_SUB * c
    assert ctx_len % rows == 0 and t_all % rows == 0
    gw = SCAN_GROUP * HEAD_DIM
    nc = t_all // rows
    cc = ctx_len // rows
    back = lambda i: jnp.where(i < cc, cc - 1 - i, nc - 1 - (i - cc))
    fspec = pl.BlockSpec((None, rows, d), lambda bb, i: (bb, i, 0))
    bspec = pl.BlockSpec((None, rows, d), lambda bb, i: (bb, back(i), 0))
    out = jax.ShapeDtypeStruct((bsz, t_all, d), F32)
    return pl.pallas_call(
        _scan_kernel,
        grid=(bsz, nc),
        in_specs=[pl.BlockSpec((None, rows, 5 * d), lambda bb, i: (bb, i, 0)), fspec,
                  pl.BlockSpec((None, rows, 5 * d), lambda bb, i: (bb, back(i), 0)), bspec],
        out_specs=[fspec, bspec],
        out_shape=[out, out],
        scratch_shapes=[pltpu.VMEM((2, d // gw, gw, gw), F32)],
        compiler_params=_params(("parallel", "arbitrary")),
        name="wkv_scan",
    )(pk, lw_f, pk, lw_b)


def _group_norm(y, gmat, g, b):
    mean = _group_sum(y, gmat) * (1.0 / HEAD_DIM)
    yc = y - mean
    var = _group_sum(yc * yc, gmat) * (1.0 / HEAD_DIM)
    return yc * lax.rsqrt(var + GN_EPS) * g + b


def _rwkv_out_mlp_kernel(yf_ref, yb_ref, bonus_ref, g0_ref, g1_ref, x_ref, gt1_ref, sh2_ref,
                         sc2_ref, gt2_ref, gng_ref, gnb_ref, g2_ref, gmat_ref, wo_ref, w1_ref,
                         w2_ref, out_ref):
    gmat = gmat_ref[...]
    gng, gnb = gng_ref[...], gnb_ref[...]
    bonus = bonus_ref[...]
    o = ((_group_norm(yf_ref[...], gmat, gng, gnb) + bonus) * g0_ref[...]
         + (_group_norm(yb_ref[...], gmat, gng, gnb) + bonus) * g1_ref[...])
    y = jnp.dot(o.astype(BF16), wo_ref[...], preferred_element_type=F32)
    x1 = x_ref[...] + gt1_ref[...] * y
    out_ref[...] = _mlp_tail(x1, sh2_ref[...], sc2_ref[...], gt2_ref[...], g2_ref[...],
                             w1_ref, w2_ref)


def _rwkv_out_mlp_call(yf, yb, bonus, g0, g1, xu, modsr, ctx_len, gng, gnb, g2, gmat, w_o, w1, w2):
    bsz, t_all, d = xu.shape
    tm = ROW_TILE
    ct = ctx_len // tm
    ns = (t_all - ctx_len) // tm
    tile = pl.BlockSpec((None, tm, d), lambda b, t: (b, t + ct, 0))
    lat_mod = lambda slot: pl.BlockSpec((None, 1, d), lambda b, t: ((6 + slot) * HALO + b, 0, 0))
    consts = [gng, gnb, g2, gmat, w_o, w1, w2]
    return pl.pallas_call(
        _rwkv_out_mlp_kernel,
        grid=(bsz, ns),
        in_specs=[tile] * 6 + [lat_mod(2), lat_mod(3), lat_mod(4), lat_mod(5)]
        + [_const_spec(c.shape) for c in consts],
        out_specs=pl.BlockSpec((None, tm, d), lambda b, t: (b, t, 0)),
        out_shape=jax.ShapeDtypeStruct((bsz, t_all - ctx_len, d), F32),
        compiler_params=_params(("parallel", "arbitrary")),
        name="rwkv_out_mlp",
    )(yf, yb, bonus, g0, g1, xu, modsr, modsr, modsr, modsr, *consts)


def _rope_tables(ctx_len, s):
    half = HEAD_DIM // 4
    lane = jnp.arange(HEAD_DIM)
    freqs = ROPE_BASE ** (-(lane % half).astype(F32) / half)
    pos = jnp.arange(s)
    coord = jnp.where(lane[None, :] < HEAD_DIM // 2, (pos // GRID_W)[:, None], (pos % GRID_W)[:, None])
    ang = coord.astype(F32) * freqs[None, :]
    sign = jnp.where((lane % (2 * half)) < half, -1.0, 1.0)
    cos = jnp.concatenate([jnp.ones((ctx_len, HEAD_DIM), F32), jnp.cos(ang)], axis=0)
    sin = jnp.concatenate([jnp.zeros((ctx_len, HEAD_DIM), F32), jnp.sin(ang) * sign], axis=0)
    return jnp.tile(cos, (1, 2)), jnp.tile(sin, (1, 2))


def _block_diag2(w):
    z = jnp.zeros_like(w[0])
    return jnp.concatenate([jnp.concatenate([w[0], z], axis=1),
                            jnp.concatenate([z, w[1]], axis=1)], axis=0)


def kernel(x, c, ctx, c_ctx, ada_w, ada_b, norm1_g, norm2_g, mlp_w1, mlp_w2, att_w_qkv, att_q_gain, att_k_gain, att_sink, att_w_o, rwkv_mu, rwkv_w_rkv, rwkv_w0, rwkv_w1, rwkv_w2, rwkv_a0, rwkv_a1, rwkv_a2, rwkv_g1, rwkv_g2, rwkv_k_k, rwkv_k_a, rwkv_r_k, rwkv_gn_g, rwkv_gn_b, rwkv_w_o):
    bsz, s, d = x.shape
    l = ctx.shape[1]
    depth = ada_w.shape[0]
    assert depth == 2 and bsz < HALO and d % LANES == 0
    n_q = att_w_o.shape[1]
    n_kv = (att_w_qkv.shape[2] - n_q) // 2
    gmat = _group_matrix()

    cc = jnp.concatenate([c, c_ctx[None, :], jnp.zeros((HALO - bsz - 1, d), F32)], axis=0)
    mods = _ada_mods(cc, ada_w, ada_b)
    modsr = mods.reshape(depth, HALO, 6, d).transpose(0, 2, 1, 3).reshape(depth * 6 * HALO, 1, d)

    cos, sin = _rope_tables(l, s)
    q, kt, v = _qkv_call(x, ctx, modsr, norm1_g[0][None], att_w_qkv[0].astype(BF16),
                         jnp.tile(att_q_gain[0], 2)[None], jnp.tile(att_k_gain[0], 2)[None],
                         cos, sin, gmat, n_q, n_kv)
    o = _attn_call(att_sink[0], q, kt, v, l)
    xu = _attn_out_mlp_call(o, x, ctx, modsr, norm2_g[0][None], att_w_o[0].astype(BF16),
                            mlp_w1[0].astype(BF16), mlp_w2[0].astype(BF16))

    row = lambda a: a.reshape(1, -1)
    w1c = jnp.concatenate([rwkv_w1[0, 0], rwkv_w1[0, 1]], axis=1).astype(BF16)
    g1c = jnp.concatenate([rwkv_g1[0, 0], rwkv_g1[0, 1]], axis=1).astype(BF16)
    feats = _rwkv_feat_call(
        xu, modsr, bsz, l, norm1_g[1][None], rwkv_mu[0], rwkv_w_rkv[0].astype(BF16),
        row(rwkv_w0[0]), w1c, _block_diag2(rwkv_w2[0]).astype(BF16), row(rwkv_a0[0]),
        rwkv_a1[0].astype(BF16), rwkv_a2[0].astype(BF16), g1c,
        _block_diag2(rwkv_g2[0]).astype(BF16), row(rwkv_k_k[0]), row(rwkv_k_a[0]),
        row(rwkv_r_k[0]), gmat)
    pk, lw0, lw1, bonus, gate0, gate1 = feats
    y_fwd, y_bwd = _scan_call(pk, lw0, lw1, l)
    return _rwkv_out_mlp_call(y_fwd, y_bwd, bonus, gate0, gate1, xu, modsr, l, row(rwkv_gn_g[0]),
                              row(rwkv_gn_b[0]), norm2_g[1][None], gmat, rwkv_w_o[0].astype(BF16),
                              mlp_w1[1].astype(BF16), mlp_w2[1].astype(BF16))
```
